```python
import math
import jax
import jax.numpy as jnp
from jax import lax
import numpy as np

D_MODEL = 4096
BATCH = 4
SEQ = 2048
DEPTH = 1
DEC_BATCH = 32
DEC_SEQ = 8
PAST_LEN = 8192
PAGE_SIZE = 128

MIX_DIFF = D_MODEL // 2
MIX_NSA = D_MODEL - MIX_DIFF
DIFF_HALF = 64
DIFF_QK = 2 * DIFF_HALF
DIFF_V = 2 * DIFF_HALF
N_HEADS_DIFF = MIX_DIFF // DIFF_V
NSA_HD = 128
N_HEADS_NSA = MIX_NSA // NSA_HD
NSA_KV = 2
NSA_HPG = N_HEADS_NSA // NSA_KV
BLK = 64
N_SELECT = 16
WINDOW = 512
D_FF = 4 * D_MODEL
Q_BLOCK = 128
EPS = 1e-6
NEG = -1e30
BIG_SEL = 1e4
PROJ_SIZES = (N_HEADS_DIFF * DIFF_QK, N_HEADS_DIFF * DIFF_QK, N_HEADS_DIFF * DIFF_V,
              N_HEADS_NSA * NSA_HD) + (NSA_KV * NSA_HD,) * 6 + (3 * N_HEADS_NSA,)
N_IN = sum(PROJ_SIZES)

kernel_name = 'hybrid_diffattn_nsa_adaln_decode_step'


def rms_norm(x, g):
    x32 = x.astype(jnp.float32)
    y = x32 * lax.rsqrt(jnp.mean(x32 * x32, axis=-1, keepdims=True) + EPS)
    return (y * g.astype(jnp.float32)).astype(x.dtype)


def alibi_slopes(n):
    return jnp.asarray(np.exp2(-8.0 * np.arange(1, n + 1) / n), dtype=jnp.float32)


def nsa_slopes():
    return alibi_slopes(N_HEADS_NSA).reshape(NSA_KV, NSA_HPG, 1, 1)


def masked_softmax(s, mask):
    s = jnp.where(mask, s, NEG)
    m = jnp.max(s, axis=-1, keepdims=True)
    e = jnp.where(mask, jnp.exp(s - m), 0.0)
    return e / jnp.maximum(jnp.sum(e, axis=-1, keepdims=True), 1e-30)


def to_qblocks(a):
    b, t = a.shape[:2]
    return jnp.swapaxes(a.reshape((b, t // Q_BLOCK, Q_BLOCK) + a.shape[2:]), 0, 1)


def from_qblocks(a):
    a = jnp.swapaxes(a, 0, 1)
    return a.reshape((a.shape[0], a.shape[1] * a.shape[2]) + a.shape[3:])


def gather_pages(cache, layer, page_table):
    g = cache[layer, page_table]
    return g.reshape((g.shape[0], g.shape[1] * g.shape[2]) + g.shape[3:])


def last_rows(a, n):
    a = jnp.pad(a, ((0, 0), (n, 0)) + ((0, 0),) * (a.ndim - 2))
    return a[:, a.shape[1] - n:]


def ada_modulation(c, w, b):
    m = (jax.nn.silu(c) @ w + b)[:, None, :]
    return jnp.split(m, 6, axis=-1)


def modulate(x, g, shift, scale):
    return rms_norm(x, g) * (1 + scale) + shift


def split_projection(z):
    b, t = z.shape[:2]
    offs = [int(o) for o in np.cumsum(PROJ_SIZES)[:-1]]
    dq, dk, dv, nq, kc, vc, ks, vs, kw, vw, gt = jnp.split(z, offs, axis=-1)
    hd = lambda a, h, d: a.reshape(b, t, h, d)
    return (hd(dq, N_HEADS_DIFF, DIFF_QK), hd(dk, N_HEADS_DIFF, DIFF_QK), hd(dv, N_HEADS_DIFF, DIFF_V),
            hd(nq, N_HEADS_NSA, NSA_HD),
            hd(kc, NSA_KV, NSA_HD), hd(vc, NSA_KV, NSA_HD),
            hd(ks, NSA_KV, NSA_HD), hd(vs, NSA_KV, NSA_HD),
            hd(kw, NSA_KV, NSA_HD), hd(vw, NSA_KV, NSA_HD),
            jax.nn.sigmoid(gt.reshape(b, t, N_HEADS_NSA, 3)))


def diff_attention(q, k_list, v_list, pos_q, pos_k_list, lam, lam_init, subln_g):
    scale = DIFF_HALF ** -0.5
    q1, q2 = q[..., :DIFF_HALF], q[..., DIFF_HALF:]
    s1 = jnp.concatenate([jnp.einsum('bqhd,bkhd->bhqk', q1, k[..., :DIFF_HALF],
                                     preferred_element_type=jnp.float32) for k in k_list], axis=-1)
    s2 = jnp.concatenate([jnp.einsum('bqhd,bkhd->bhqk', q2, k[..., DIFF_HALF:],
                                     preferred_element_type=jnp.float32) for k in k_list], axis=-1)
    pos_k = jnp.concatenate(pos_k_list)
    dist = pos_q[:, None] - pos_k[None, :]
    mask = dist >= 0
    bias = -alibi_slopes(N_HEADS_DIFF)[:, None, None] * dist.astype(jnp.float32)
    p1 = masked_softmax(s1 * scale + bias, mask)
    p2 = masked_softmax(s2 * scale + bias, mask)
    w = p1 - lam * p2
    o = None
    start = 0
    for v in v_list:
        n = v.shape[1]
        part = jnp.einsum('bhqk,bkhd->bqhd', w[..., start:start + n].astype(v.dtype), v)
        o = part if o is None else o + part
        start += n
    return rms_norm(o, subln_g) * (1.0 - lam_init)


def compress(x, pos_emb, w1, w2):
    b, l, g, d = x.shape
    nc = l // BLK
    xb = x[:, :nc * BLK].reshape(b, nc, BLK, g, d) + pos_emb[:, None, :]
    xb = jnp.transpose(xb, (0, 1, 3, 2, 4)).reshape(b, nc, g, BLK * d)
    return jax.nn.silu(xb @ w1) @ w2


def gqa_probs(q, k, pos_q, pos_k, mask):
    b, t = q.shape[:2]
    qg = q.reshape(b, t, NSA_KV, NSA_HPG, NSA_HD)
    s = jnp.einsum('btgjd,bsgd->bgjts', qg, k, preferred_element_type=jnp.float32) * NSA_HD ** -0.5
    dist = (pos_q[:, None] - pos_k[None, :]).astype(jnp.float32)
    return masked_softmax(s - nsa_slopes() * dist, mask)


def gqa_out(p, v):
    o = jnp.einsum('bgjts,bsgd->btgjd', p.astype(v.dtype), v)
    return o.reshape(o.shape[0], o.shape[1], N_HEADS_NSA, NSA_HD)


def select_blocks(p_cmp, pos_q, n_blocks):
    imp = jnp.sum(p_cmp, axis=2)
    imp = jnp.pad(imp, ((0, 0), (0, 0), (0, 0), (0, n_blocks - imp.shape[-1])))
    blk = jnp.arange(n_blocks)[None, :]
    complete = blk * BLK + (BLK - 1) <= pos_q[:, None]
    cur = blk == (pos_q // BLK)[:, None]
    imp = jnp.where(cur, BIG_SEL, jnp.where(blk == 0, 0.5 * BIG_SEL, jnp.where(complete, imp, NEG)))
    return lax.top_k(imp, min(N_SELECT, n_blocks))[1]


def sel_blocks(k):
    b, l = k.shape[:2]
    return jnp.transpose(k.reshape(b, l // BLK, BLK, NSA_KV, NSA_HD), (0, 3, 1, 2, 4))


def selected_attention(q, pos_q, idx, kb, vb):
    b, t = q.shape[:2]
    bi = jnp.arange(b)[:, None, None, None]
    gi = jnp.arange(NSA_KV)[None, :, None, None]
    n_keys = idx.shape[-1] * BLK
    kg = kb[bi, gi, idx].reshape(b, NSA_KV, t, n_keys, NSA_HD)
    vg = vb[bi, gi, idx].reshape(b, NSA_KV, t, n_keys, NSA_HD)
    pos_k = (idx[..., None] * BLK + jnp.arange(BLK)).reshape(b, NSA_KV, t, n_keys)
    qg = q.reshape(b, t, NSA_KV, NSA_HPG, NSA_HD)
    s = jnp.einsum('btgjd,bgtsd->bgjts', qg, kg, preferred_element_type=jnp.float32) * NSA_HD ** -0.5
    dist = (pos_q[None, None, :, None] - pos_k)[:, :, None]
    p = masked_softmax(s - nsa_slopes() * dist.astype(jnp.float32), dist >= 0)
    o = jnp.einsum('bgjts,bgtsd->btgjd', p.astype(vg.dtype), vg)
    return o.reshape(b, t, N_HEADS_NSA, NSA_HD)


def window_prompt(q, k, v):
    t = q.shape[1]
    pad = ((0, 0), (WINDOW, 0), (0, 0), (0, 0))
    kp, vp = jnp.pad(k, pad), jnp.pad(v, pad)

    def one_block(args):
        qb, start = args
        kb = lax.dynamic_slice_in_dim(kp, start, WINDOW + Q_BLOCK, axis=1)
        vb = lax.dynamic_slice_in_dim(vp, start, WINDOW + Q_BLOCK, axis=1)
        pos_q = start + jnp.arange(Q_BLOCK)
        pos_k = start - WINDOW + jnp.arange(WINDOW + Q_BLOCK)
        dist = pos_q[:, None] - pos_k[None, :]
        mask = (dist >= 0) & (dist <= WINDOW) & (pos_k >= 0)[None, :]
        return gqa_out(gqa_probs(qb, kb, pos_q, pos_k, mask), vb)

    starts = jnp.arange(t // Q_BLOCK) * Q_BLOCK
    return from_qblocks(lax.map(one_block, (to_qblocks(q), starts)))


def nsa_prompt(q, kc, vc, ks, vs, kw, vw, gates, cmp):
    b, t = q.shape[:2]
    pos = jnp.arange(t)
    kcb = compress(kc, cmp[0], cmp[1], cmp[2])
    vcb = compress(vc, cmp[3], cmp[4], cmp[5])
    nc = kcb.shape[1]
    blk_end = jnp.arange(nc) * BLK + (BLK - 1)
    p_cmp = gqa_probs(q, kcb, pos, blk_end, blk_end[None, :] <= pos[:, None])
    o_cmp = gqa_out(p_cmp, vcb)
    idx = select_blocks(p_cmp, pos, nc)
    idx_b = jnp.transpose(idx.reshape(b, NSA_KV, t // Q_BLOCK, Q_BLOCK, idx.shape[-1]), (2, 0, 1, 3, 4))
    kb, vb = sel_blocks(ks), sel_blocks(vs)
    o_slc = from_qblocks(lax.map(lambda a: selected_attention(a[0], a[1], a[2], kb, vb),
                                 (to_qblocks(q), pos.reshape(-1, Q_BLOCK), idx_b)))
    o_win = window_prompt(q, kw, vw)
    return gates[..., 0:1] * o_cmp + gates[..., 1:2] * o_slc + gates[..., 2:3] * o_win


def nsa_sample(q, kc, vc, ks, vs, kw, vw, gates, kc_past, vc_past, ks_past, vs_past, win_k, win_v, cmp):
    t = q.shape[1]
    p_len = kc_past.shape[1]
    total = p_len + t
    pos_q = p_len + jnp.arange(t)
    kcb = compress(jnp.concatenate([kc_past, kc], axis=1), cmp[0], cmp[1], cmp[2])
    vcb = compress(jnp.concatenate([vc_past, vc], axis=1), cmp[3], cmp[4], cmp[5])
    nc = kcb.shape[1]
    blk_end = jnp.arange(nc) * BLK + (BLK - 1)
    p_cmp = gqa_probs(q, kcb, pos_q, blk_end, blk_end[None, :] <= pos_q[:, None])
    o_cmp = gqa_out(p_cmp, vcb)
    ns = -(-total // BLK)
    pad = ((0, 0), (0, ns * BLK - total), (0, 0), (0, 0))
    ks_all = jnp.pad(jnp.concatenate([ks_past, ks], axis=1), pad)
    vs_all = jnp.pad(jnp.concatenate([vs_past, vs], axis=1), pad)
    idx = select_blocks(p_cmp, pos_q, ns)
    o_slc = selected_attention(q, pos_q, idx, sel_blocks(ks_all), sel_blocks(vs_all))
    w = win_k.shape[1]
    kw_all = jnp.concatenate([win_k, kw], axis=1)
    vw_all = jnp.concatenate([win_v, vw], axis=1)
    pos_k = p_len - w + jnp.arange(w + t)
    dist = pos_q[:, None] - pos_k[None, :]
    mask = (dist >= 0) & (dist <= WINDOW)
    o_win = gqa_out(gqa_probs(q, kw_all, pos_q, pos_k, mask), vw_all)
    o = gates[..., 0:1] * o_cmp + gates[..., 1:2] * o_slc + gates[..., 2:3] * o_win
    return o, kw_all[:, t:], vw_all[:, t:]


def finish_layer(x, o_diff, o_nsa, gate1, shift2, scale2, gate2, norm2_g, w_out, w_up, w_down):
    b, t = x.shape[:2]
    mix = jnp.concatenate([o_diff.reshape(b, t, -1), o_nsa.reshape(b, t, -1)], axis=-1) @ w_out
    x = x + gate1 * mix
    h = modulate(x, norm2_g, shift2, scale2)
    return x + gate2 * (jnp.square(jax.nn.relu(h @ w_up)) @ w_down)


def setup_inputs(seed: int = 0) -> dict:
    key = jax.random.key(seed)
    k = jax.random.split(key, 40)
    f32 = jnp.float32
    n_pages = PAST_LEN // PAGE_SIZE
    n_phys = (DEC_BATCH * n_pages * 5 + 3) // 4
    w_buf = min(WINDOW, PAST_LEN)
    nrm = lambda i, shape, s=1.0: jax.random.normal(k[i], shape, f32) * s
    page_table = jax.random.permutation(k[0], n_phys)[: DEC_BATCH * n_pages].reshape(DEC_BATCH, n_pages).astype(jnp.int32)
    return {
        'x_prompt': nrm(1, (BATCH, SEQ, D_MODEL)),
        'x_sample': nrm(2, (DEC_BATCH, DEC_SEQ, D_MODEL)),
        'c_prompt': nrm(3, (BATCH, D_MODEL)),
        'c_sample': nrm(4, (DEC_BATCH, D_MODEL)),
        'cache_diff_k': nrm(5, (DEPTH, n_phys, PAGE_SIZE, N_HEADS_DIFF, DIFF_QK)),
        'cache_diff_v': nrm(6, (DEPTH, n_phys, PAGE_SIZE, N_HEADS_DIFF, DIFF_V)),
        'cache_cmp_k': nrm(7, (DEPTH, n_phys, PAGE_SIZE, NSA_KV, NSA_HD)),
        'cache_cmp_v': nrm(8, (DEPTH, n_phys, PAGE_SIZE, NSA_KV, NSA_HD)),
        'cache_slc_k': nrm(9, (DEPTH, n_phys, PAGE_SIZE, NSA_KV, NSA_HD)),
        'cache_slc_v': nrm(10, (DEPTH, n_phys, PAGE_SIZE, NSA_KV, NSA_HD)),
        'state_win_k': nrm(11, (DEPTH, DEC_BATCH, w_buf, NSA_KV, NSA_HD)),
        'state_win_v': nrm(12, (DEPTH, DEC_BATCH, w_buf, NSA_KV, NSA_HD)),
        'page_table': page_table,
        'norm1_g': 1.0 + nrm(13, (DEPTH, D_MODEL), 0.05),
        'norm2_g': 1.0 + nrm(14, (DEPTH, D_MODEL), 0.05),
        'ada_w': nrm(15, (DEPTH, D_MODEL, 6 * D_MODEL), 0.5 * D_MODEL ** -0.5),
        'ada_b': nrm(16, (DEPTH, 6 * D_MODEL), 0.02),
        'w_in': nrm(17, (DEPTH, D_MODEL, N_IN), D_MODEL ** -0.5),
        'w_out': nrm(18, (DEPTH, D_MODEL, D_MODEL), D_MODEL ** -0.5),
        'diff_lq1': nrm(19, (DEPTH, DIFF_HALF), 0.1),
        'diff_lk1': nrm(20, (DEPTH, DIFF_HALF), 0.1),
        'diff_lq2': nrm(21, (DEPTH, DIFF_HALF), 0.1),
        'diff_lk2': nrm(22, (DEPTH, DIFF_HALF), 0.1),
        'diff_subln_g': 1.0 + nrm(23, (DEPTH, DIFF_V), 0.05),
        'cmp_k_pos': nrm(24, (DEPTH, BLK, NSA_HD), 0.1),
        'cmp_k_w1': nrm(25, (DEPTH, BLK * NSA_HD, NSA_HD), (BLK * NSA_HD) ** -0.5),
        'cmp_k_w2': nrm(26, (DEPTH, NSA_HD, NSA_HD), NSA_HD ** -0.5),
        'cmp_v_pos': nrm(27, (DEPTH, BLK, NSA_HD), 0.1),
        'cmp_v_w1': nrm(28, (DEPTH, BLK * NSA_HD, NSA_HD), (BLK * NSA_HD) ** -0.5),
        'cmp_v_w2': nrm(29, (DEPTH, NSA_HD, NSA_HD), NSA_HD ** -0.5),
        'w_up': nrm(30, (DEPTH, D_MODEL, D_FF), D_MODEL ** -0.5),
        'w_down': nrm(31, (DEPTH, D_FF, D_MODEL), D_FF ** -0.5),
        'final_g': 1.0 + nrm(32, (D_MODEL,), 0.05),
    }


def reference(x_prompt, x_sample, c_prompt, c_sample, cache_diff_k, cache_diff_v, cache_cmp_k, cache_cmp_v,
              cache_slc_k, cache_slc_v, state_win_k, state_win_v, page_table, norm1_g, norm2_g, ada_w, ada_b,
              w_in, w_out, diff_lq1, diff_lk1, diff_lq2, diff_lk2, diff_subln_g, cmp_k_pos, cmp_k_w1, cmp_k_w2,
              cmp_v_pos, cmp_v_w1, cmp_v_w2, w_up, w_down, final_g):
    t_p = x_prompt.shape[1]
    t_s = x_sample.shape[1]
    past = page_table.shape[1] * cache_diff_k.shape[2]
    w_buf = state_win_k.shape[2]
    pos_p = jnp.arange(t_p)
    pos_s = past + jnp.arange(t_s)
    pos_past = jnp.arange(past)
    xp, xs = x_prompt, x_sample
    acc_p = [[] for _ in range(8)]
    acc_s = [[] for _ in range(8)]
    for l in range(DEPTH):
        lam_init = 0.8 - 0.6 * math.exp(-0.3 * l)
        lam = (jnp.exp(jnp.sum(diff_lq1[l].astype(jnp.float32) * diff_lk1[l].astype(jnp.float32)))
               - jnp.exp(jnp.sum(diff_lq2[l].astype(jnp.float32) * diff_lk2[l].astype(jnp.float32))) + lam_init)
        cmp = (cmp_k_pos[l], cmp_k_w1[l], cmp_k_w2[l], cmp_v_pos[l], cmp_v_w1[l], cmp_v_w2[l])
        subln = diff_subln_g[l]

        sh1, sc1, g1, sh2, sc2, g2 = ada_modulation(c_prompt, ada_w[l], ada_b[l])
        dq, dk, dv, nq, kc, vc, ks, vs, kw, vw, gates = split_projection(modulate(xp, norm1_g[l], sh1, sc1) @ w_in[l])
        o_diff = from_qblocks(lax.map(
            lambda a: diff_attention(a[0], [dk], [dv], a[1], [pos_p], lam, lam_init, subln),
            (to_qblocks(dq), pos_p.reshape(-1, Q_BLOCK))))
        o_nsa = nsa_prompt(nq, kc, vc, ks, vs, kw, vw, gates, cmp)
        xp = finish_layer(xp, o_diff, o_nsa, g1, sh2, sc2, g2, norm2_g[l], w_out[l], w_up[l], w_down[l])
        for lst, a in zip(acc_p, (dk, dv, kc, vc, ks, vs, last_rows(kw, w_buf), last_rows(vw, w_buf))):
            lst.append(a)

        sh1, sc1, g1, sh2, sc2, g2 = ada_modulation(c_sample, ada_w[l], ada_b[l])
        dq, dk, dv, nq, kc, vc, ks, vs, kw, vw, gates = split_projection(modulate(xs, norm1_g[l], sh1, sc1) @ w_in[l])
        o_diff = diff_attention(dq, [gather_pages(cache_diff_k, l, page_table), dk],
                                [gather_pages(cache_diff_v, l, page_table), dv],
                                pos_s, [pos_past, pos_s], lam, lam_init, subln)
        o_nsa, win_k_new, win_v_new = nsa_sample(
            nq, kc, vc, ks, vs, kw, vw, gates,
            gather_pages(cache_cmp_k, l, page_table), gather_pages(cache_cmp_v, l, page_table),
            gather_pages(cache_slc_k, l, page_table), gather_pages(cache_slc_v, l, page_table),
            state_win_k[l], state_win_v[l], cmp)
        xs = finish_layer(xs, o_diff, o_nsa, g1, sh2, sc2, g2, norm2_g[l], w_out[l], w_up[l], w_down[l])
        for lst, a in zip(acc_s, (dk, dv, kc, vc, ks, vs, win_k_new, win_v_new)):
            lst.append(a)

    y_prompt = rms_norm(xp, final_g)
    y_sample = rms_norm(xs, final_g)
    (new_diff_k_prompt, new_diff_v_prompt, new_cmp_k_prompt, new_cmp_v_prompt,
     new_slc_k_prompt, new_slc_v_prompt, new_win_k_prompt, new_win_v_prompt) = [jnp.stack(a) for a in acc_p]
    (new_diff_k_sample, new_diff_v_sample, new_cmp_k_sample, new_cmp_v_sample,
     new_slc_k_sample, new_slc_v_sample, new_win_k_sample, new_win_v_sample) = [jnp.stack(a) for a in acc_s]
    return (y_prompt, y_sample,
            new_diff_k_prompt, new_diff_v_prompt, new_cmp_k_prompt, new_cmp_v_prompt,
            new_slc_k_prompt, new_slc_v_prompt, new_win_k_prompt, new_win_v_prompt,
            new_diff_k_sample, new_diff_v_sample, new_cmp_k_sample, new_cmp_v_sample,
            new_slc_k_sample, new_slc_v_sample, new_win_k_sample, new_win_v_sample)
```

```python
import functools
import math

import numpy as np
import jax
import jax.numpy as jnp
from jax import lax
from jax.experimental import pallas as pl
from jax.experimental.pallas import tpu as pltpu

F32 = jnp.float32
BF16 = jnp.bfloat16

DIFF_HALF = 64
HEAD_DIM = 128
NSA_KV = 2
BLK = 64
N_SELECT = 16
WINDOW = 512
EPS = 1e-6
NEG = -1e30
BIG_SEL = 1e4
LANES = 128
VMEM_LIMIT_BYTES = 56 * 1024 * 1024


def _params(*sem):
    return pltpu.CompilerParams(dimension_semantics=sem, vmem_limit_bytes=VMEM_LIMIT_BYTES)


def _alibi_slopes(n):
    return jnp.asarray(np.exp2(-8.0 * np.arange(1, n + 1) / n), dtype=F32)


def _silu(x):
    return x * jax.nn.sigmoid(x)


def _smem_spec():
    return pl.BlockSpec(memory_space=pltpu.SMEM)


def _ada_body(c_ref, w_ref, b_ref, o_ref):
    a = _silu(c_ref[...]).astype(BF16)
    o_ref[...] = jnp.dot(a, w_ref[...].astype(BF16), preferred_element_type=F32) + b_ref[...]


def ada_modulation(c, w, b, tn=512):
    m, k = c.shape
    n = w.shape[1]
    tn = min(tn, n)
    return pl.pallas_call(
        _ada_body,
        grid=(n // tn,),
        in_specs=[pl.BlockSpec((m, k), lambda j: (0, 0)),
                  pl.BlockSpec((k, tn), lambda j: (0, j)),
                  pl.BlockSpec((1, tn), lambda j: (0, j))],
        out_specs=pl.BlockSpec((m, tn), lambda j: (0, j)),
        out_shape=jax.ShapeDtypeStruct((m, n), F32),
        compiler_params=_params("arbitrary"),
        name="ada_modulation",
    )(c, w, b.reshape(1, n))


def _rms(x, g):
    return x * lax.rsqrt(jnp.mean(x * x, axis=-1, keepdims=True) + EPS) * g


def _modulate_body(x_ref, sh_ref, sc_ref, g_ref, h_ref):
    h = _rms(x_ref[...], g_ref[...]) * (1.0 + sc_ref[...]) + sh_ref[...]
    h_ref[...] = h.reshape(h_ref.shape).astype(h_ref.dtype)


def _mid_body(x_ref, mix_ref, g1_ref, sh_ref, sc_ref, g_ref, x1_ref, h_ref):
    x1 = x_ref[...] + g1_ref[...] * mix_ref[...].reshape(x_ref.shape)
    x1_ref[...] = x1
    h = _rms(x1, g_ref[...]) * (1.0 + sc_ref[...]) + sh_ref[...]
    h_ref[...] = h.reshape(h_ref.shape).astype(h_ref.dtype)


def _final_body(x_ref, ffn_ref, g2_ref, g_ref, y_ref):
    x2 = x_ref[...] + g2_ref[...] * ffn_ref[...].reshape(x_ref.shape)
    y_ref[...] = _rms(x2, g_ref[...])


def _row_tiles(b, t):
    if t >= 256:
        return 1, 256
    return b, t


def _row_specs(b, t, d):
    bb, tt = _row_tiles(b, t)
    nt = t // tt
    grid = (b // bb, nt)
    x3 = pl.BlockSpec((bb, tt, d), lambda i, j: (i, j, 0))
    mod = pl.BlockSpec((bb, 1, d), lambda i, j: (i, 0, 0))
    gain = pl.BlockSpec((1, 1, d), lambda i, j: (0, 0, 0))
    flat = pl.BlockSpec((bb * tt, d), lambda i, j: (i * nt + j, 0))
    return grid, x3, mod, gain, flat


def modulate(x, shift, scale, g):
    b, t, d = x.shape
    grid, x3, mod, gain, flat = _row_specs(b, t, d)
    return pl.pallas_call(
        _modulate_body, grid=grid,
        in_specs=[x3, mod, mod, gain], out_specs=flat,
        out_shape=jax.ShapeDtypeStruct((b * t, d), BF16),
        compiler_params=_params("arbitrary", "arbitrary"),
        name="modulate",
    )(x, shift, scale, g.reshape(1, 1, d))


def residual_modulate(x, mix, gate1, shift, scale, g):
    b, t, d = x.shape
    grid, x3, mod, gain, flat = _row_specs(b, t, d)
    return pl.pallas_call(
        _mid_body, grid=grid,
        in_specs=[x3, flat, mod, mod, mod, gain], out_specs=[x3, flat],
        out_shape=[jax.ShapeDtypeStruct((b, t, d), F32), jax.ShapeDtypeStruct((b * t, d), BF16)],
        compiler_params=_params("arbitrary", "arbitrary"),
        name="residual_modulate",
    )(x, mix, gate1, shift, scale, g.reshape(1, 1, d))


def residual_final_norm(x, ffn, gate2, g):
    b, t, d = x.shape
    grid, x3, mod, gain, flat = _row_specs(b, t, d)
    return pl.pallas_call(
        _final_body, grid=grid,
        in_specs=[x3, flat, mod, gain], out_specs=x3,
        out_shape=jax.ShapeDtypeStruct((b, t, d), F32),
        compiler_params=_params("arbitrary", "arbitrary"),
        name="residual_final_norm",
    )(x, ffn, gate2, g.reshape(1, 1, d))


def _mm_body(a_ref, w_ref, o_ref, *scratch, nk, relu2):
    part = jnp.dot(a_ref[...], w_ref[...].astype(BF16), preferred_element_type=F32)

    def finish(r):
        if relu2:
            r = jnp.square(jnp.maximum(r, 0.0))
        o_ref[...] = r.astype(o_ref.dtype)

    if nk == 1:
        finish(part)
    else:
        acc_ref, = scratch
        k = pl.program_id(2)

        @pl.when(k == 0)
        def _():
            acc_ref[...] = part

        @pl.when(k > 0)
        def _():
            acc_ref[...] += part

        @pl.when(k == nk - 1)
        def _():
            finish(acc_ref[...])


def matmul(a, w, *, col0=0, ncols=None, tm=1024, tn=512, tk=None, out_dtype=F32, relu2=False):
    m, k = a.shape
    ncols = w.shape[1] - col0 if ncols is None else ncols
    tm, tn = min(tm, m), min(tn, ncols)
    tk = k if tk is None else min(tk, k)
    assert m % tm == 0 and ncols % tn == 0 and k % tk == 0 and col0 % tn == 0
    nk, jb = k // tk, col0 // tn
    return pl.pallas_call(
        functools.partial(_mm_body, nk=nk, relu2=relu2),
        grid=(m // tm, ncols // tn, nk),
        in_specs=[pl.BlockSpec((tm, tk), lambda i, j, kk: (i, kk)),
                  pl.BlockSpec((tk, tn), lambda i, j, kk: (kk, jb + j))],
        out_specs=pl.BlockSpec((tm, tn), lambda i, j, kk: (i, j)),
        out_shape=jax.ShapeDtypeStruct((m, ncols), out_dtype),
        scratch_shapes=[pltpu.VMEM((tm, tn), F32)] if nk > 1 else [],
        compiler_params=_params("arbitrary", "arbitrary", "arbitrary"),
        name="matmul",
    )(a, w)


def _kv_gate_body(a_ref, w_ref, *o_refs, n_kv, n_gate):
    j = pl.program_id(1)
    r = jnp.dot(a_ref[...], w_ref[...].astype(BF16), preferred_element_type=F32)
    for n in range(n_kv):
        for half in range(2):
            @pl.when(j == 2 * n + half)
            def _(n=n, half=half):
                o_refs[n][:, half * LANES:(half + 1) * LANES] = r

    @pl.when(j == 2 * n_kv)
    def _():
        col = lax.broadcasted_iota(jnp.int32, r.shape, 1)
        o_refs[n_kv][...] = jnp.where(col < n_gate, jax.nn.sigmoid(r), 0.0)


def kv_gate_projection(a, w, col0, n_kv, n_gate, tm=1024):
    m, k = a.shape
    tm = min(tm, m)
    jb = col0 // LANES
    kv = pl.BlockSpec((tm, 2 * LANES), lambda i, j: (i, 0))
    return pl.pallas_call(
        functools.partial(_kv_gate_body, n_kv=n_kv, n_gate=n_gate),
        grid=(m // tm, 2 * n_kv + 1),
        in_specs=[pl.BlockSpec((tm, k), lambda i, j: (i, 0)),
                  pl.BlockSpec((k, LANES), lambda i, j: (0, jb + j))],
        out_specs=[kv] * n_kv + [pl.BlockSpec((tm, LANES), lambda i, j: (i, 0))],
        out_shape=[jax.ShapeDtypeStruct((m, 2 * LANES), F32)] * n_kv + [jax.ShapeDtypeStruct((m, LANES), F32)],
        compiler_params=_params("arbitrary", "arbitrary"),
        name="kv_gate_projection",
    )(a, w)


def _nt_dot(a, b):
    return lax.dot_general(a, b, (((1,), (1,)), ((), ())), preferred_element_type=F32)


def _softmax_update(carry, s, mask, v):
    m, l, acc = carry
    s = jnp.where(mask, s, NEG)
    m_new = jnp.maximum(m, jnp.max(s, axis=-1, keepdims=True))
    p = jnp.where(mask, jnp.exp(s - m_new), 0.0)
    alpha = jnp.exp(m - m_new)
    l = alpha * l + jnp.sum(p, axis=-1, keepdims=True)
    rows = acc.shape[0]
    pv = jnp.dot(p.reshape(rows, p.shape[-1]).astype(BF16), v, preferred_element_type=F32)
    acc = alpha.reshape(rows, 1) * acc + pv
    return m_new, l, acc


def _softmax_init(lead, rows_total, d):
    return (jnp.full(lead + (1,), NEG, F32), jnp.zeros(lead + (1,), F32), jnp.zeros((rows_total, d), F32))


def _softmax_finish(carry):
    _, l, acc = carry
    return acc / jnp.maximum(l.reshape(acc.shape[0], 1), 1e-30)


def _lambda_full(lq1, lk1, lq2, lk2, lam_init):
    a = jnp.sum(lq1 * lk1, axis=-1, keepdims=True)
    b = jnp.sum(lq2 * lk2, axis=-1, keepdims=True)
    return jnp.exp(a) - jnp.exp(b) + lam_init


def _diff_qcat(q):
    lane = lax.broadcasted_iota(jnp.int32, q.shape, 1)
    qs = q * (DIFF_HALF ** -0.5)
    first = jnp.where(lane < DIFF_HALF, qs, 0.0)
    second = jnp.where(lane >= DIFF_HALF, qs, 0.0)
    return jnp.concatenate([first, second], axis=0).astype(BF16)


def _diff_finish(o12, lam, lam_init, subln):
    t = o12.shape[0] // 2
    o = o12[:t] - lam * o12[t:]
    return _rms(o, subln) * (1.0 - lam_init)


def _diff_prompt_body(slope_ref, q_ref, k_ref, v_ref, lq1, lk1, lq2, lk2, subln_ref, o_ref, *, tq, ck, lam_init):
    h, qi = pl.program_id(1), pl.program_id(2)
    slope = slope_ref[h]
    qcat = _diff_qcat(q_ref[0])
    q0 = qi * tq
    row = lax.broadcasted_iota(jnp.int32, (2 * tq, ck), 0)
    col = lax.broadcasted_iota(jnp.int32, (2 * tq, ck), 1)
    rel = q0 + jnp.where(row >= tq, row - tq, row) - col

    def chunk(c, carry):
        k0 = pl.multiple_of(c * ck, ck)
        k = k_ref[0, pl.ds(k0, ck), :].astype(BF16)
        v = v_ref[0, pl.ds(k0, ck), :].astype(BF16)
        dist = rel - k0
        s = _nt_dot(qcat, k) - slope * dist.astype(F32)
        return _softmax_update(carry, s, dist >= 0, v)

    n_chunks = (q0 + tq + ck - 1) // ck
    carry = lax.fori_loop(0, n_chunks, chunk, _softmax_init((2 * tq,), 2 * tq, HEAD_DIM))
    lam = _lambda_full(lq1[...], lk1[...], lq2[...], lk2[...], lam_init)
    o_ref[0] = _diff_finish(_softmax_finish(carry), lam, lam_init, subln_ref[...]).astype(o_ref.dtype)


def diff_attention_prompt(q, k, v, lams, subln, lam_init, tq=256, ck=512):
    b, t, hd = q.shape
    n_heads = hd // HEAD_DIM
    tq, ck = min(tq, t), min(ck, t)
    assert ck % tq == 0
    qspec = pl.BlockSpec((1, tq, HEAD_DIM), lambda bi, h, qi: (bi, qi, h))
    kvspec = pl.BlockSpec((1, t, HEAD_DIM), lambda bi, h, qi: (bi, 0, h))
    vec = pl.BlockSpec((1, DIFF_HALF), lambda bi, h, qi: (0, 0))
    return pl.pallas_call(
        functools.partial(_diff_prompt_body, tq=tq, ck=ck, lam_init=lam_init),
        grid=(b, n_heads, t // tq),
        in_specs=[_smem_spec(), qspec, kvspec, kvspec, vec, vec, vec, vec,
                  pl.BlockSpec((1, HEAD_DIM), lambda bi, h, qi: (0, 0))],
        out_specs=qspec,
        out_shape=jax.ShapeDtypeStruct((b, t, hd), BF16),
        compiler_params=_params("arbitrary", "arbitrary", "arbitrary"),
        name="diff_attention_prompt",
    )(_alibi_slopes(n_heads), q, k, v, *lams, subln)


def _diff_sample_body(pt_ref, slope_ref, q_ref, kn_ref, vn_ref, lq1, lk1, lq2, lk2, subln_ref, *rest,
                      n_pages_step, n_heads, page, t_new, past, lam_init):
    k_pages = rest[:n_pages_step]
    v_pages = rest[n_pages_step:2 * n_pages_step]
    o_ref = rest[2 * n_pages_step]
    qcat_ref, m_ref, l_ref, acc_ref = rest[2 * n_pages_step + 1:]
    c, n_c = pl.program_id(1), pl.num_programs(1)
    rows = 2 * t_new

    @pl.when(c == 0)
    def _():
        for h in range(n_heads):
            qcat_ref[h] = _diff_qcat(q_ref[0, :, h * HEAD_DIM:(h + 1) * HEAD_DIM])
        m_ref[...] = jnp.full(m_ref.shape, NEG, F32)
        l_ref[...] = jnp.zeros(l_ref.shape, F32)
        acc_ref[...] = jnp.zeros(acc_ref.shape, F32)

    row = lax.broadcasted_iota(jnp.int32, (rows, page), 0)
    col = lax.broadcasted_iota(jnp.int32, (rows, page), 1)
    tq = jnp.where(row >= t_new, row - t_new, row)
    all_keys = jnp.full((rows, page), True)

    def head(h, _):
        slope = slope_ref[h]
        qcat = qcat_ref[h]
        carry = (m_ref[h], l_ref[h], acc_ref[h])
        for i in range(n_pages_step):
            k = k_pages[i][0, pl.ds(h, page, stride=n_heads), :].astype(BF16)
            v = v_pages[i][0, pl.ds(h, page, stride=n_heads), :].astype(BF16)
            dist = past + tq - ((c * n_pages_step + i) * page + col)
            s = _nt_dot(qcat, k) - slope * dist.astype(F32)
            carry = _softmax_update(carry, s, all_keys, v)
        m_ref[h], l_ref[h], acc_ref[h] = carry
        return 0

    lax.fori_loop(0, n_heads, head, 0)

    @pl.when(c == n_c - 1)
    def _():
        lam = _lambda_full(lq1[...], lk1[...], lq2[...], lk2[...], lam_init)
        pad = jnp.zeros((page - t_new, HEAD_DIM), F32)
        dist = tq - col
        for h in range(n_heads):
            lanes = slice(h * HEAD_DIM, (h + 1) * HEAD_DIM)
            k = jnp.concatenate([kn_ref[0, :, lanes], pad], axis=0).astype(BF16)
            v = jnp.concatenate([vn_ref[0, :, lanes], pad], axis=0).astype(BF16)
            s = _nt_dot(qcat_ref[h], k) - slope_ref[h] * dist.astype(F32)
            carry = _softmax_update((m_ref[h], l_ref[h], acc_ref[h]), s, (dist >= 0) & (col < t_new), v)
            o = _diff_finish(_softmax_finish(carry), lam, lam_init, subln_ref[...])
            o_ref[0, :, lanes] = o.astype(o_ref.dtype)


def diff_attention_sample(q, k_new, v_new, cache_k, cache_v, page_table, lams, subln, lam_init, pages_per_step=8):
    b, t, hd = q.shape
    n_heads = hd // HEAD_DIM
    n_pages = page_table.shape[1]
    page = cache_k.shape[1] // n_heads
    pps = min(pages_per_step, n_pages)
    assert n_pages % pps == 0 and t <= page
    tok = pl.BlockSpec((1, t, hd), lambda bi, c, pt: (bi, 0, 0))
    vec = pl.BlockSpec((1, DIFF_HALF), lambda bi, c, pt: (0, 0))

    def page_spec(i):
        return pl.BlockSpec((1, page * n_heads, HEAD_DIM), lambda bi, c, pt: (pt[bi, c * pps + i], 0, 0))

    pages = [page_spec(i) for i in range(pps)]
    rows = 2 * t
    grid_spec = pltpu.PrefetchScalarGridSpec(
        num_scalar_prefetch=1,
        grid=(b, n_pages // pps),
        in_specs=[_smem_spec(), tok, tok, tok, vec, vec, vec, vec,
                  pl.BlockSpec((1, HEAD_DIM), lambda bi, c, pt: (0, 0))] + pages + pages,
        out_specs=tok,
        scratch_shapes=[pltpu.VMEM((n_heads, rows, HEAD_DIM), BF16),
                        pltpu.VMEM((n_heads, rows, 1), F32),
                        pltpu.VMEM((n_heads, rows, 1), F32),
                        pltpu.VMEM((n_heads, rows, HEAD_DIM), F32)])
    return pl.pallas_call(
        functools.partial(_diff_sample_body, n_pages_step=pps, n_heads=n_heads, page=page, t_new=t,
                          past=n_pages * page, lam_init=lam_init),
        grid_spec=grid_spec,
        out_shape=jax.ShapeDtypeStruct((b, t, hd), BF16),
        compiler_params=_params("arbitrary", "arbitrary"),
        name="diff_attention_sample",
    )(page_table, _alibi_slopes(n_heads), q, k_new, v_new, *lams, subln, *([cache_k] * pps), *([cache_v] * pps))


def _compress_body(*refs, n_ops, rows_op, nc, nc_pad):
    page_refs = refs[:n_ops]
    pos_ref, w1_ref, w2_ref, o_ref, x_ref = refs[n_ops:]
    c, n_c = pl.program_id(1), pl.num_programs(1)
    for i, ref in enumerate(page_refs):
        r0 = pl.multiple_of((c * n_ops + i) * rows_op, rows_op)
        for g in range(NSA_KV):
            x_ref[g, pl.ds(r0, rows_op), :] = ref[0, :, g * HEAD_DIM:(g + 1) * HEAD_DIM]

    @pl.when(c == n_c - 1)
    def _():
        w1 = w1_ref[...].astype(BF16)
        w2 = w2_ref[...].astype(BF16)
        for g in range(NSA_KV):
            pieces = [(x_ref[g, pl.ds(j, nc, stride=BLK), :] + pos_ref[j:j + 1, :]).astype(BF16)
                      for j in range(BLK)]
            a = jnp.concatenate(pieces, axis=1)
            mid = _silu(jnp.dot(a, w1, preferred_element_type=F32))
            out = jnp.dot(mid.astype(BF16), w2, preferred_element_type=F32)
            o_ref[0, g, 0:nc, :] = out
            if nc_pad > nc:
                o_ref[0, g, nc:nc_pad, :] = jnp.zeros((nc_pad - nc, HEAD_DIM), F32)


def compress(page_arrays, page_specs, grid, prefetch, length, pos, w1, w2, nc_pad):
    n_ops = len(page_arrays)
    rows_op = length // (grid[1] * n_ops)
    nc = length // BLK
    n_pre = len(prefetch)
    const = lambda shape: pl.BlockSpec(shape, lambda bi, c, *pt: (0,) * len(shape))
    grid_spec = pltpu.PrefetchScalarGridSpec(
        num_scalar_prefetch=n_pre, grid=grid,
        in_specs=list(page_specs) + [const(pos.shape), const(w1.shape), const(w2.shape)],
        out_specs=pl.BlockSpec((1, NSA_KV, nc_pad, HEAD_DIM), lambda bi, c, *pt: (bi, 0, 0, 0)),
        scratch_shapes=[pltpu.VMEM((NSA_KV, length, HEAD_DIM), F32)])

    def body(*refs):
        _compress_body(*refs[n_pre:], n_ops=n_ops, rows_op=rows_op, nc=nc, nc_pad=nc_pad)

    return pl.pallas_call(
        body, grid_spec=grid_spec,
        out_shape=jax.ShapeDtypeStruct((grid[0], NSA_KV, nc_pad, HEAD_DIM), F32),
        compiler_params=_params("arbitrary", "arbitrary"),
        name="compress",
    )(*prefetch, *page_arrays, pos, w1, w2)


def compress_prompt(x, pos, w1, w2, nc_pad):
    b, t, w = x.shape
    spec = pl.BlockSpec((1, t, w), lambda bi, c: (bi, 0, 0))
    return compress([x], [spec], (b, 1), [], t, pos, w1, w2, nc_pad)


def compress_paged(cache, page_table, pos, w1, w2, nc_pad, pages_per_step=8):
    b, n_pages = page_table.shape
    page, w = cache.shape[1:]
    pps = min(pages_per_step, n_pages)

    def spec(i):
        return pl.BlockSpec((1, page, w), lambda bi, c, pt: (pt[bi, c * pps + i], 0, 0))

    return compress([cache] * pps, [spec(i) for i in range(pps)], (b, n_pages // pps), [page_table],
                    n_pages * page, pos, w1, w2, nc_pad)


def _select_blocks(imp, pos_q, n_blocks):
    t, w = imp.shape
    blk = lax.broadcasted_iota(jnp.int32, (t, w), 1)
    complete = blk * BLK + (BLK - 1) <= pos_q
    cur = blk == pos_q // BLK
    val = jnp.where(cur, BIG_SEL, jnp.where(blk == 0, 0.5 * BIG_SEL, jnp.where(complete, imp, NEG)))
    val = jnp.where(blk < n_blocks, val, -3e38)
    rank = jnp.zeros((t, w), jnp.int32)
    for i in range(n_blocks):
        vi = val[:, i:i + 1]
        ahead = (vi > val) | ((vi == val) & (blk > i))
        rank = rank + ahead.astype(jnp.int32)
    return (rank < min(N_SELECT, n_blocks)) & (blk < n_blocks)


def _expand_blocks(sel, first_block, n_keys):
    w = sel.shape[1]
    blk = lax.broadcasted_iota(jnp.int32, (w, n_keys), 0)
    key = lax.broadcasted_iota(jnp.int32, (w, n_keys), 1)
    onehot = jnp.where(blk == first_block + key // BLK, 1.0, 0.0).astype(BF16)
    return jnp.dot(sel, onehot, preferred_element_type=F32)


def _head_rows(q_ref, g, hpg):
    parts = [q_ref[0, :, (g * hpg + j) * HEAD_DIM:(g * hpg + j + 1) * HEAD_DIM] for j in range(hpg)]
    return (jnp.concatenate(parts, axis=0) * (HEAD_DIM ** -0.5)).astype(BF16)


def _group_slopes(slope_ref, g, hpg):
    j = lax.broadcasted_iota(jnp.int32, (hpg, 1, 1), 0)
    out = jnp.zeros((hpg, 1, 1), F32)
    for jj in range(hpg):
        out = jnp.where(j == jj, slope_ref[g * hpg + jj], out)
    return out


def _compressed_branch(q, kcb, vcb, slopes, pos_q, nc, hpg):
    t = pos_q.shape[0]
    w = kcb.shape[0]
    blk_end = lax.broadcasted_iota(jnp.int32, (t, w), 1) * BLK + (BLK - 1)
    dist = pos_q - blk_end
    mask = ((dist >= 0) & (blk_end < nc * BLK))[None]
    s = _nt_dot(q, kcb.astype(BF16)).reshape(hpg, t, w) - slopes * dist.astype(F32)[None]
    s = jnp.where(mask, s, NEG)
    e = jnp.where(mask, jnp.exp(s - jnp.max(s, axis=-1, keepdims=True)), 0.0)
    p = e / jnp.maximum(jnp.sum(e, axis=-1, keepdims=True), 1e-30)
    o = jnp.dot(p.reshape(hpg * t, w).astype(BF16), vcb.astype(BF16), preferred_element_type=F32)
    return o, jnp.sum(p, axis=0)


def _merge_heads(o_ref, gates, g, hpg, o_cmp, o_slc, o_win):
    t = o_cmp.shape[0] // hpg
    for j in range(hpg):
        head = g * hpg + j
        rows = slice(j * t, (j + 1) * t)
        o = (gates[:, 3 * head:3 * head + 1] * o_cmp[rows]
             + gates[:, 3 * head + 1:3 * head + 2] * o_slc[rows]
             + gates[:, 3 * head + 2:3 * head + 3] * o_win[rows])
        o_ref[0, :, head * HEAD_DIM:(head + 1) * HEAD_DIM] = o.astype(o_ref.dtype)


def _nsa_prompt_body(slope_ref, q_ref, kcb_ref, vcb_ref, ks_ref, vs_ref, kw_ref, vw_ref, gate_ref, o_ref,
                     *, tq, ck, cw, nc, hpg):
    qi = pl.program_id(1)
    q0 = qi * tq
    pos_q = q0 + lax.broadcasted_iota(jnp.int32, (tq, 1), 0)
    gates = gate_ref[0]
    rel_s = pos_q - lax.broadcasted_iota(jnp.int32, (tq, ck), 1)
    rel_w = pos_q - lax.broadcasted_iota(jnp.int32, (tq, cw), 1)
    for g in range(NSA_KV):
        lanes = pl.ds(g * HEAD_DIM, HEAD_DIM)
        q = _head_rows(q_ref, g, hpg)
        slopes = _group_slopes(slope_ref, g, hpg)
        o_cmp, imp = _compressed_branch(q, kcb_ref[0, g], vcb_ref[0, g], slopes, pos_q, nc, hpg)
        sel = jnp.where(_select_blocks(imp, pos_q, nc), 1.0, 0.0).astype(BF16)

        def slc_chunk(c, carry):
            k0 = pl.multiple_of(c * ck, ck)
            k = ks_ref[0, pl.ds(k0, ck), lanes].astype(BF16)
            v = vs_ref[0, pl.ds(k0, ck), lanes].astype(BF16)
            dist = rel_s - k0
            mask = ((_expand_blocks(sel, k0 // BLK, ck) > 0.5) & (dist >= 0))[None]
            s = _nt_dot(q, k).reshape(hpg, tq, ck) - slopes * dist.astype(F32)[None]
            return _softmax_update(carry, s, jnp.broadcast_to(mask, s.shape), v)

        init = _softmax_init((hpg, tq), hpg * tq, HEAD_DIM)
        o_slc = _softmax_finish(lax.fori_loop(0, (q0 + tq + ck - 1) // ck, slc_chunk, init))

        def win_chunk(c, carry):
            k0 = pl.multiple_of(c * cw, cw)
            k = kw_ref[0, pl.ds(k0, cw), lanes].astype(BF16)
            v = vw_ref[0, pl.ds(k0, cw), lanes].astype(BF16)
            dist = rel_w - k0
            mask = ((dist >= 0) & (dist <= WINDOW))[None]
            s = _nt_dot(q, k).reshape(hpg, tq, cw) - slopes * dist.astype(F32)[None]
            return _softmax_update(carry, s, jnp.broadcast_to(mask, s.shape), v)

        first = jnp.maximum(q0 - WINDOW, 0) // cw
        o_win = _softmax_finish(lax.fori_loop(first, (q0 + tq + cw - 1) // cw, win_chunk, init))
        _merge_heads(o_ref, gates, g, hpg, o_cmp, o_slc, o_win)


def nsa_attention_prompt(q, kcb, vcb, ks, vs, kw, vw, gates, tq=128, ck=512, cw=128):
    b, t, hd = q.shape
    n_heads = hd // HEAD_DIM
    hpg = n_heads // NSA_KV
    tq, ck, cw = min(tq, t), min(ck, t), min(cw, t)
    assert ck % tq == 0 and tq % cw == 0 and WINDOW % cw == 0
    tok = pl.BlockSpec((1, tq, hd), lambda bi, qi: (bi, qi, 0))
    cmp_spec = pl.BlockSpec((1,) + kcb.shape[1:], lambda bi, qi: (bi, 0, 0, 0))
    seq = pl.BlockSpec((1, t, NSA_KV * HEAD_DIM), lambda bi, qi: (bi, 0, 0))
    return pl.pallas_call(
        functools.partial(_nsa_prompt_body, tq=tq, ck=ck, cw=cw, nc=t // BLK, hpg=hpg),
        grid=(b, t // tq),
        in_specs=[_smem_spec(), tok, cmp_spec, cmp_spec, seq, seq, seq, seq,
                  pl.BlockSpec((1, tq, LANES), lambda bi, qi: (bi, qi, 0))],
        out_specs=tok,
        out_shape=jax.ShapeDtypeStruct((b, t, hd), BF16),
        compiler_params=_params("arbitrary", "arbitrary"),
        name="nsa_attention_prompt",
    )(_alibi_slopes(n_heads), q, kcb, vcb, ks, vs, kw, vw, gates)


def _nsa_sample_body(pt_ref, slope_ref, q_ref, kcb_ref, vcb_ref, ksn_ref, vsn_ref, kwn_ref, vwn_ref,
                     wk_ref, wv_ref, gate_ref, *rest, pps, page, t_new, past, hpg, sel_w):
    k_pages = rest[:pps]
    v_pages = rest[pps:2 * pps]
    o_ref = rest[2 * pps]
    q_s, ocmp_s, sel_s, m_s, l_s, acc_s = rest[2 * pps + 1:]
    c, n_c = pl.program_id(1), pl.num_programs(1)
    rows = hpg * t_new
    nc = past // BLK
    n_blocks = -(-(past + t_new) // BLK)
    pos_q = past + lax.broadcasted_iota(jnp.int32, (t_new, 1), 0)
    keys_step = pps * page

    @pl.when(c == 0)
    def _():
        for g in range(NSA_KV):
            q = _head_rows(q_ref, g, hpg)
            q_s[g] = q
            slopes = _group_slopes(slope_ref, g, hpg)
            o_cmp, imp = _compressed_branch(q, kcb_ref[0, g], vcb_ref[0, g], slopes, pos_q, nc, hpg)
            ocmp_s[g] = o_cmp
            imp = jnp.concatenate([imp, jnp.zeros((t_new, sel_w - imp.shape[1]), F32)], axis=1)
            sel_s[g] = jnp.where(_select_blocks(imp, pos_q, n_blocks), 1.0, 0.0)
        m_s[...] = jnp.full(m_s.shape, NEG, F32)
        l_s[...] = jnp.zeros(l_s.shape, F32)
        acc_s[...] = jnp.zeros(acc_s.shape, F32)

    rel = pos_q - lax.broadcasted_iota(jnp.int32, (t_new, keys_step), 1)
    for g in range(NSA_KV):
        lanes = pl.ds(g * HEAD_DIM, HEAD_DIM)
        slopes = _group_slopes(slope_ref, g, hpg)
        q = q_s[g]
        k0 = c * keys_step
        dist = rel - k0
        mask = (_expand_blocks(sel_s[g].astype(BF16), k0 // BLK, keys_step) > 0.5)[None]
        k = jnp.concatenate([k_pages[i][0, :, lanes] for i in range(pps)], axis=0).astype(BF16)
        v = jnp.concatenate([v_pages[i][0, :, lanes] for i in range(pps)], axis=0).astype(BF16)
        s = _nt_dot(q, k).reshape(hpg, t_new, keys_step) - slopes * dist.astype(F32)[None]
        carry = _softmax_update((m_s[g], l_s[g], acc_s[g]), s, jnp.broadcast_to(mask, s.shape), v)
        m_s[g], l_s[g], acc_s[g] = carry

    @pl.when(c == n_c - 1)
    def _():
        gates = gate_ref[0]
        pad = jnp.zeros((page - t_new, HEAD_DIM), F32)
        col = lax.broadcasted_iota(jnp.int32, (t_new, page), 1)
        dist_new = (pos_q - past) - col
        for g in range(NSA_KV):
            lanes = pl.ds(g * HEAD_DIM, HEAD_DIM)
            slopes = _group_slopes(slope_ref, g, hpg)
            q = q_s[g]
            k = jnp.concatenate([ksn_ref[0, :, lanes], pad], axis=0).astype(BF16)
            v = jnp.concatenate([vsn_ref[0, :, lanes], pad], axis=0).astype(BF16)
            sel_new = _expand_blocks(sel_s[g].astype(BF16), nc, page) > 0.5
            mask = (sel_new & (dist_new >= 0) & (col < t_new))[None]
            s = _nt_dot(q, k).reshape(hpg, t_new, page) - slopes * dist_new.astype(F32)[None]
            carry = _softmax_update((m_s[g], l_s[g], acc_s[g]), s, jnp.broadcast_to(mask, s.shape), v)
            o_slc = _softmax_finish(carry)
            w_buf = wk_ref.shape[1]
            colw = lax.broadcasted_iota(jnp.int32, (t_new, w_buf), 1)
            dist_w = pos_q - (past - w_buf + colw)
            carry = _softmax_init((hpg, t_new), rows, HEAD_DIM)
            sw = _nt_dot(q, wk_ref[0, :, lanes].astype(BF16)).reshape(hpg, t_new, w_buf)
            sw = sw - slopes * dist_w.astype(F32)[None]
            mask_w = ((dist_w >= 0) & (dist_w <= WINDOW))[None]
            carry = _softmax_update(carry, sw, jnp.broadcast_to(mask_w, sw.shape),
                                    wv_ref[0, :, lanes].astype(BF16))
            k = jnp.concatenate([kwn_ref[0, :, lanes], pad], axis=0).astype(BF16)
            v = jnp.concatenate([vwn_ref[0, :, lanes], pad], axis=0).astype(BF16)
            mask_n = ((dist_new >= 0) & (dist_new <= WINDOW) & (col < t_new))[None]
            sn = _nt_dot(q, k).reshape(hpg, t_new, page) - slopes * dist_new.astype(F32)[None]
            carry = _softmax_update(carry, sn, jnp.broadcast_to(mask_n, sn.shape), v)
            o_win = _softmax_finish(carry)
            _merge_heads(o_ref, gates, g, hpg, ocmp_s[g], o_slc, o_win)


def nsa_attention_sample(q, kcb, vcb, ks_new, vs_new, kw_new, vw_new, win_k, win_v, gates,
                         cache_k, cache_v, page_table, pages_per_step=8):
    b, t, hd = q.shape
    n_heads = hd // HEAD_DIM
    hpg = n_heads // NSA_KV
    n_pages = page_table.shape[1]
    page = cache_k.shape[1]
    past = n_pages * page
    pps = min(pages_per_step, n_pages)
    n_blocks = -(-(past + t) // BLK)
    sel_w = -(-n_blocks // LANES) * LANES
    assert n_pages % pps == 0 and t <= page and page % BLK == 0 and kcb.shape[2] * BLK >= past
    width = NSA_KV * HEAD_DIM
    tok = pl.BlockSpec((1, t, hd), lambda bi, c, pt: (bi, 0, 0))
    new = pl.BlockSpec((1, t, width), lambda bi, c, pt: (bi, 0, 0))
    cmp_spec = pl.BlockSpec((1,) + kcb.shape[1:], lambda bi, c, pt: (bi, 0, 0, 0))
    win = pl.BlockSpec((1,) + win_k.shape[1:], lambda bi, c, pt: (bi, 0, 0))

    def page_spec(i):
        return pl.BlockSpec((1, page, width), lambda bi, c, pt: (pt[bi, c * pps + i], 0, 0))

    pages = [page_spec(i) for i in range(pps)]
    rows = hpg * t
    grid_spec = pltpu.PrefetchScalarGridSpec(
        num_scalar_prefetch=1,
        grid=(b, n_pages // pps),
        in_specs=[_smem_spec(), tok, cmp_spec, cmp_spec, new, new, new, new, win, win,
                  pl.BlockSpec((1, t, LANES), lambda bi, c, pt: (bi, 0, 0))] + pages + pages,
        out_specs=tok,
        scratch_shapes=[pltpu.VMEM((NSA_KV, rows, HEAD_DIM), BF16),
                        pltpu.VMEM((NSA_KV, rows, HEAD_DIM), F32),
                        pltpu.VMEM((NSA_KV, t, sel_w), F32),
                        pltpu.VMEM((NSA_KV, hpg, t, 1), F32),
                        pltpu.VMEM((NSA_KV, hpg, t, 1), F32),
                        pltpu.VMEM((NSA_KV, rows, HEAD_DIM), F32)])
    return pl.pallas_call(
        functools.partial(_nsa_sample_body, pps=pps, page=page, t_new=t, past=past, hpg=hpg, sel_w=sel_w),
        grid_spec=grid_spec,
        out_shape=jax.ShapeDtypeStruct((b, t, hd), BF16),
        compiler_params=_params("arbitrary", "arbitrary"),
        name="nsa_attention_sample",
    )(page_table, _alibi_slopes(n_heads), q, kcb, vcb, ks_new, vs_new, kw_new, vw_new, win_k, win_v, gates,
      *([cache_k] * pps), *([cache_v] * pps))


def _last_rows(a, n):
    a = jnp.pad(a, ((0, 0), (n, 0)) + ((0, 0),) * (a.ndim - 2))
    return a[:, a.shape[1] - n:]


def _project(x, mods, norm_g, w_in, mix_diff, mix_nsa):
    b, t, d = x.shape
    h = modulate(x, mods[0], mods[1], norm_g)
    seg = lambda col0, n: matmul(h, w_in, col0=col0, ncols=n).reshape(b, t, n)
    dq, dk, dv = seg(0, mix_diff), seg(mix_diff, mix_diff), seg(2 * mix_diff, mix_diff)
    nq = seg(3 * mix_diff, mix_nsa)
    n_gate = 3 * (mix_nsa // HEAD_DIM)
    outs = kv_gate_projection(h, w_in, 3 * mix_diff + mix_nsa, 6, n_gate)
    kvs = [o.reshape(b, t, NSA_KV * HEAD_DIM) for o in outs[:6]]
    gates = outs[6].reshape(b, t, LANES)
    return dq, dk, dv, nq, kvs, gates


def _finish(x, o_diff, o_nsa, mods, norm2_g, w_out, w_up, w_down, final_g):
    b, t, d = x.shape
    mix_in = jnp.concatenate([o_diff, o_nsa], axis=-1).reshape(b * t, -1)
    mix = matmul(mix_in, w_out)
    x1, h2 = residual_modulate(x, mix, mods[2], mods[3], mods[4], norm2_g)
    u = matmul(h2, w_up, out_dtype=BF16, relu2=True)
    ffn = matmul(u, w_down, tn=1024, tk=1024)
    return residual_final_norm(x1, ffn, mods[5], final_g)


def kernel(x_prompt, x_sample, c_prompt, c_sample, cache_diff_k, cache_diff_v, cache_cmp_k, cache_cmp_v, cache_slc_k, cache_slc_v, state_win_k, state_win_v, page_table, norm1_g, norm2_g, ada_w, ada_b, w_in, w_out, diff_lq1, diff_lk1, diff_lq2, diff_lk2, diff_subln_g, cmp_k_pos, cmp_k_w1, cmp_k_w2, cmp_v_pos, cmp_v_w1, cmp_v_w2, w_up, w_down, final_g):
    depth = ada_w.shape[0]
    assert depth == 1
    l = 0
    bp, tp, d = x_prompt.shape
    bs, ts, _ = x_sample.shape
    n_phys, page, n_heads_diff, _ = cache_diff_k.shape[1:]
    mix_diff = n_heads_diff * HEAD_DIM
    mix_nsa = d - mix_diff
    w_buf = state_win_k.shape[2]
    width = NSA_KV * HEAD_DIM
    lam_init = 0.8 - 0.6 * math.exp(-0.3 * l)
    lams = [a[l].reshape(1, DIFF_HALF) for a in (diff_lq1, diff_lk1, diff_lq2, diff_lk2)]
    subln = diff_subln_g[l].reshape(1, HEAD_DIM)

    n_c = bp + bs
    n_c_pad = -(-n_c // 16) * 16
    c_all = jnp.concatenate([c_prompt, c_sample, jnp.zeros((n_c_pad - n_c, d), F32)], axis=0)
    mod = ada_modulation(c_all, ada_w[l], ada_b[l]).reshape(n_c_pad, 6, 1, d)
    mods_p = [mod[:bp, i] for i in range(6)]
    mods_s = [mod[bp:n_c, i] for i in range(6)]

    dq, dk_p, dv_p, nq, kvs_p, gates = _project(x_prompt, mods_p, norm1_g[l], w_in[l], mix_diff, mix_nsa)
    o_diff = diff_attention_prompt(dq, dk_p, dv_p, lams, subln, lam_init)
    nc_pad = LANES
    kcb = compress_prompt(kvs_p[0], cmp_k_pos[l], cmp_k_w1[l], cmp_k_w2[l], nc_pad)
    vcb = compress_prompt(kvs_p[1], cmp_v_pos[l], cmp_v_w1[l], cmp_v_w2[l], nc_pad)
    o_nsa = nsa_attention_prompt(nq, kcb, vcb, kvs_p[2], kvs_p[3], kvs_p[4], kvs_p[5], gates)
    y_prompt = _finish(x_prompt, o_diff, o_nsa, mods_p, norm2_g[l], w_out[l], w_up[l], w_down[l], final_g)

    dq, dk_s, dv_s, nq, kvs_s, gates = _project(x_sample, mods_s, norm1_g[l], w_in[l], mix_diff, mix_nsa)
    flat_diff = lambda c: c[l].reshape(n_phys, page * n_heads_diff, HEAD_DIM)
    o_diff = diff_attention_sample(dq, dk_s, dv_s, flat_diff(cache_diff_k), flat_diff(cache_diff_v),
                                   page_table, lams, subln, lam_init)
    flat_nsa = lambda c: c[l].reshape(n_phys, page, width)
    kcb = compress_paged(flat_nsa(cache_cmp_k), page_table, cmp_k_pos[l], cmp_k_w1[l], cmp_k_w2[l], nc_pad)
    vcb = compress_paged(flat_nsa(cache_cmp_v), page_table, cmp_v_pos[l], cmp_v_w1[l], cmp_v_w2[l], nc_pad)
    win_k = state_win_k[l].reshape(bs, w_buf, width)
    win_v = state_win_v[l].reshape(bs, w_buf, width)
    o_nsa = nsa_attention_sample(nq, kcb, vcb, kvs_s[2], kvs_s[3], kvs_s[4], kvs_s[5], win_k, win_v, gates,
                                 flat_nsa(cache_slc_k), flat_nsa(cache_slc_v), page_table)
    y_sample = _finish(x_sample, o_diff, o_nsa, mods_s, norm2_g[l], w_out[l], w_up[l], w_down[l], final_g)

    heads = lambda a, b, t, h: a.reshape(1, b, t, h, HEAD_DIM)
    out_p = [heads(dk_p, bp, tp, n_heads_diff), heads(dv_p, bp, tp, n_heads_diff)]
    out_p += [heads(a, bp, tp, NSA_KV) for a in kvs_p[:4]]
    out_p += [heads(_last_rows(a, w_buf), bp, w_buf, NSA_KV) for a in kvs_p[4:]]
    out_s = [heads(dk_s, bs, ts, n_heads_diff), heads(dv_s, bs, ts, n_heads_diff)]
    out_s += [heads(a, bs, ts, NSA_KV) for a in kvs_s[:4]]
    out_s += [heads(jnp.concatenate([w, a], axis=1)[:, ts:], bs, w_buf, NSA_KV)
              for w, a in ((win_k, kvs_s[4]), (win_v, kvs_s[5]))]
    return (y_prompt, y_sample, *out_p, *out_s)
```

```python
import functools
import math

import numpy as np
import jax
import jax.numpy as jnp
from jax import lax
from jax.experimental import pallas as pl
from jax.experimental.pallas import tpu as pltpu

F32 = jnp.float32
BF16 = jnp.bfloat16

DIFF_HALF = 64
HEAD_DIM = 128
NSA_KV = 2
BLK = 64
N_SELECT = 16
WINDOW = 512
EPS = 1e-6
NEG = -1e30
BIG_SEL = 1e4
LOG2E = math.log2(math.e)
LANES = 128
SUBLANES = 8
VMEM_LIMIT_BYTES = 56 * 1024 * 1024


def _params(*sem):
    return pltpu.CompilerParams(dimension_semantics=sem, vmem_limit_bytes=VMEM_LIMIT_BYTES)


def _alibi_slopes(n):
    return jnp.asarray(np.exp2(-8.0 * np.arange(1, n + 1) / n), dtype=F32)


def _silu(x):
    return x * jax.nn.sigmoid(x)


def _smem_spec():
    return pl.BlockSpec(memory_space=pltpu.SMEM)


def _ada_body(c_ref, w_ref, b_ref, o_ref):
    a = _silu(c_ref[...]).astype(BF16)
    o_ref[...] = jnp.dot(a, w_ref[...].astype(BF16), preferred_element_type=F32) + b_ref[...]


def ada_modulation(c, w, b, tn=512):
    m, k = c.shape
    n = w.shape[1]
    tn = min(tn, n)
    return pl.pallas_call(
        _ada_body,
        grid=(n // tn,),
        in_specs=[pl.BlockSpec((m, k), lambda j: (0, 0)),
                  pl.BlockSpec((k, tn), lambda j: (0, j)),
                  pl.BlockSpec((1, tn), lambda j: (0, j))],
        out_specs=pl.BlockSpec((m, tn), lambda j: (0, j)),
        out_shape=jax.ShapeDtypeStruct((m, n), F32),
        compiler_params=_params("arbitrary"),
        name="ada_modulation",
    )(c, w, b.reshape(1, n))


def _rms(x, g):
    return x * lax.rsqrt(jnp.mean(x * x, axis=-1, keepdims=True) + EPS) * g


def _modulate_body(x_ref, sh_ref, sc_ref, g_ref, h_ref):
    h = _rms(x_ref[...], g_ref[...]) * (1.0 + sc_ref[...]) + sh_ref[...]
    h_ref[...] = h.reshape(h_ref.shape).astype(h_ref.dtype)


def _mid_body(x_ref, mix_ref, g1_ref, sh_ref, sc_ref, g_ref, x1_ref, h_ref):
    x1 = x_ref[...] + g1_ref[...] * mix_ref[...].reshape(x_ref.shape)
    x1_ref[...] = x1
    h = _rms(x1, g_ref[...]) * (1.0 + sc_ref[...]) + sh_ref[...]
    h_ref[...] = h.reshape(h_ref.shape).astype(h_ref.dtype)


def _final_body(x_ref, ffn_ref, g2_ref, g_ref, y_ref):
    x2 = x_ref[...] + g2_ref[...] * ffn_ref[...].reshape(x_ref.shape)
    y_ref[...] = _rms(x2, g_ref[...])


def _row_tiles(b, t):
    if t >= 256:
        return 1, 256
    return b, t


def _row_specs(b, t, d):
    bb, tt = _row_tiles(b, t)
    nt = t // tt
    grid = (b // bb, nt)
    x3 = pl.BlockSpec((bb, tt, d), lambda i, j: (i, j, 0))
    mod = pl.BlockSpec((bb, 1, d), lambda i, j: (i, 0, 0))
    gain = pl.BlockSpec((1, 1, d), lambda i, j: (0, 0, 0))
    flat = pl.BlockSpec((bb * tt, d), lambda i, j: (i * nt + j, 0))
    return grid, x3, mod, gain, flat


def modulate(x, shift, scale, g):
    b, t, d = x.shape
    grid, x3, mod, gain, flat = _row_specs(b, t, d)
    return pl.pallas_call(
        _modulate_body, grid=grid,
        in_specs=[x3, mod, mod, gain], out_specs=flat,
        out_shape=jax.ShapeDtypeStruct((b * t, d), BF16),
        compiler_params=_params("arbitrary", "arbitrary"),
        name="modulate",
    )(x, shift, scale, g.reshape(1, 1, d))


def residual_modulate(x, mix, gate1, shift, scale, g):
    b, t, d = x.shape
    grid, x3, mod, gain, flat = _row_specs(b, t, d)
    return pl.pallas_call(
        _mid_body, grid=grid,
        in_specs=[x3, flat, mod, mod, mod, gain], out_specs=[x3, flat],
        out_shape=[jax.ShapeDtypeStruct((b, t, d), F32), jax.ShapeDtypeStruct((b * t, d), BF16)],
        compiler_params=_params("arbitrary", "arbitrary"),
        name="residual_modulate",
    )(x, mix, gate1, shift, scale, g.reshape(1, 1, d))


def residual_final_norm(x, ffn, gate2, g):
    b, t, d = x.shape
    grid, x3, mod, gain, flat = _row_specs(b, t, d)
    return pl.pallas_call(
        _final_body, grid=grid,
        in_specs=[x3, flat, mod, gain], out_specs=x3,
        out_shape=jax.ShapeDtypeStruct((b, t, d), F32),
        compiler_params=_params("arbitrary", "arbitrary"),
        name="residual_final_norm",
    )(x, ffn, gate2, g.reshape(1, 1, d))


def _nt_dot(a, b):
    return lax.dot_general(a, b, (((1,), (1,)), ((), ())), preferred_element_type=F32)


def _w_dot(a, w, w_rows_are_outputs):
    w = w.astype(BF16)
    return _nt_dot(a, w) if w_rows_are_outputs else jnp.dot(a, w, preferred_element_type=F32)


def _mm_body(a_ref, w_ref, o_ref, *scratch, nk, relu2, wt):
    part = _w_dot(a_ref[...], w_ref[...], wt)

    def finish(r):
        if relu2:
            r = jnp.square(jnp.maximum(r, 0.0))
        o_ref[...] = r.astype(o_ref.dtype)

    if nk == 1:
        finish(part)
    else:
        acc_ref, = scratch
        k = pl.program_id(2)

        @pl.when(k == 0)
        def _():
            acc_ref[...] = part

        @pl.when(k > 0)
        def _():
            acc_ref[...] += part

        @pl.when(k == nk - 1)
        def _():
            finish(acc_ref[...])


def matmul(a, w, *, col0=0, ncols=None, tm=1024, tn=512, tk=None, out_dtype=F32, relu2=False, wt=False):
    m, k = a.shape
    n_total = w.shape[0] if wt else w.shape[1]
    ncols = n_total - col0 if ncols is None else ncols
    tm, tn = min(tm, m), min(tn, ncols)
    tk = k if tk is None else min(tk, k)
    assert m % tm == 0 and ncols % tn == 0 and k % tk == 0 and col0 % tn == 0
    nk, jb = k // tk, col0 // tn
    if wt:
        w_spec = pl.BlockSpec((tn, tk), lambda i, j, kk: (jb + j, kk))
    else:
        w_spec = pl.BlockSpec((tk, tn), lambda i, j, kk: (kk, jb + j))
    return pl.pallas_call(
        functools.partial(_mm_body, nk=nk, relu2=relu2, wt=wt),
        grid=(m // tm, ncols // tn, nk),
        in_specs=[pl.BlockSpec((tm, tk), lambda i, j, kk: (i, kk)), w_spec],
        out_specs=pl.BlockSpec((tm, tn), lambda i, j, kk: (i, j)),
        out_shape=jax.ShapeDtypeStruct((m, ncols), out_dtype),
        scratch_shapes=[pltpu.VMEM((tm, tn), F32)] if nk > 1 else [],
        compiler_params=_params("arbitrary", "arbitrary", "arbitrary"),
        name="matmul",
    )(a, w)


def _kv_gate_body(a_ref, w_ref, *o_refs, n_kv, n_gate):
    j = pl.program_id(1)
    r = _w_dot(a_ref[...], w_ref[...], True)
    for n in range(n_kv):
        for half in range(2):
            @pl.when(j == 2 * n + half)
            def _(n=n, half=half):
                o_refs[n][:, half * LANES:(half + 1) * LANES] = r

    @pl.when(j == 2 * n_kv)
    def _():
        col = lax.broadcasted_iota(jnp.int32, r.shape, 1)
        o_refs[n_kv][...] = jnp.where(col < n_gate, jax.nn.sigmoid(r), 0.0)


def kv_gate_projection(a, w, col0, n_kv, n_gate, tm=1024):
    m, k = a.shape
    tm = min(tm, m)
    jb = col0 // LANES
    kv = pl.BlockSpec((tm, 2 * LANES), lambda i, j: (i, 0))
    return pl.pallas_call(
        functools.partial(_kv_gate_body, n_kv=n_kv, n_gate=n_gate),
        grid=(m // tm, 2 * n_kv + 1),
        in_specs=[pl.BlockSpec((tm, k), lambda i, j: (i, 0)),
                  pl.BlockSpec((LANES, k), lambda i, j: (jb + j, 0))],
        out_specs=[kv] * n_kv + [pl.BlockSpec((tm, LANES), lambda i, j: (i, 0))],
        out_shape=[jax.ShapeDtypeStruct((m, 2 * LANES), F32)] * n_kv + [jax.ShapeDtypeStruct((m, LANES), F32)],
        compiler_params=_params("arbitrary", "arbitrary"),
        name="kv_gate_projection",
    )(a, w)


def _softmax_update(carry, s, mask, v, zero_masked=True):
    m, l, acc = carry
    if mask is not None:
        s = jnp.where(mask, s, NEG)
    m_new = jnp.maximum(m, jnp.max(s, axis=-1, keepdims=True))
    p = jnp.exp2(s - m_new)
    if mask is not None and zero_masked:
        p = jnp.where(mask, p, 0.0)
    alpha = jnp.exp2(m - m_new)
    l = alpha * l + jnp.sum(p, axis=-1, keepdims=True)
    rows = acc.shape[0]
    pv = jnp.dot(p.reshape(rows, p.shape[-1]).astype(BF16), v, preferred_element_type=F32)
    acc = alpha.reshape(rows, 1) * acc + pv
    return m_new, l, acc


def _softmax_init(lead, rows_total, d):
    return (jnp.full(lead + (1,), NEG, F32), jnp.zeros(lead + (1,), F32), jnp.zeros((rows_total, d), F32))


def _softmax_finish(carry):
    _, l, acc = carry
    return acc / jnp.maximum(l.reshape(acc.shape[0], 1), 1e-30)


def _lambda_full(lq1, lk1, lq2, lk2, lam_init):
    a = jnp.sum(lq1 * lk1, axis=-1, keepdims=True)
    b = jnp.sum(lq2 * lk2, axis=-1, keepdims=True)
    return jnp.exp(a) - jnp.exp(b) + lam_init


def _diff_qcat(q):
    lane = lax.broadcasted_iota(jnp.int32, q.shape, 1)
    qs = q * (DIFF_HALF ** -0.5 * LOG2E)
    first = jnp.where(lane < DIFF_HALF, qs, 0.0)
    second = jnp.where(lane >= DIFF_HALF, qs, 0.0)
    return jnp.concatenate([first, second], axis=0).astype(BF16)


def _diff_finish(o12, lam, lam_init, subln):
    t = o12.shape[0] // 2
    o = o12[:t] - lam * o12[t:]
    return _rms(o, subln) * (1.0 - lam_init)


def _diff_prompt_body(slope_ref, q_ref, k_ref, v_ref, lq1, lk1, lq2, lk2, subln_ref, o_ref, *, tq, ck, lam_init):
    h, qi = pl.program_id(1), pl.program_id(2)
    slope = slope_ref[h] * LOG2E
    qcat = _diff_qcat(q_ref[0])
    q0 = qi * tq
    row = lax.broadcasted_iota(jnp.int32, (2 * tq, ck), 0)
    col = lax.broadcasted_iota(jnp.int32, (2 * tq, ck), 1)
    rel = q0 + jnp.where(row >= tq, row - tq, row) - col
    colf = lax.broadcasted_iota(jnp.int32, (1, ck), 1).astype(F32)

    def chunk(c, carry, causal):
        k0 = pl.multiple_of(c * ck, ck)
        k = k_ref[0, pl.ds(k0, ck), :].astype(BF16)
        v = v_ref[0, pl.ds(k0, ck), :].astype(BF16)
        s = _nt_dot(qcat, k) + slope * (colf + (k0 - q0).astype(F32))
        return _softmax_update(carry, s, (rel >= k0) if causal else None, v, zero_masked=False)

    n_full = q0 // ck
    n_chunks = (q0 + tq + ck - 1) // ck
    carry = _softmax_init((2 * tq,), 2 * tq, HEAD_DIM)
    carry = lax.fori_loop(0, n_full, functools.partial(chunk, causal=False), carry)
    carry = lax.fori_loop(n_full, n_chunks, functools.partial(chunk, causal=True), carry)
    lam = _lambda_full(lq1[...], lk1[...], lq2[...], lk2[...], lam_init)
    o_ref[0] = _diff_finish(_softmax_finish(carry), lam, lam_init, subln_ref[...]).astype(o_ref.dtype)


def diff_attention_prompt(q, k, v, lams, subln, lam_init, tq=256, ck=512):
    b, t, hd = q.shape
    n_heads = hd // HEAD_DIM
    tq, ck = min(tq, t), min(ck, t)
    assert ck % tq == 0
    qspec = pl.BlockSpec((1, tq, HEAD_DIM), lambda bi, h, qi: (bi, qi, h))
    kvspec = pl.BlockSpec((1, t, HEAD_DIM), lambda bi, h, qi: (bi, 0, h))
    vec = pl.BlockSpec((1, DIFF_HALF), lambda bi, h, qi: (0, 0))
    return pl.pallas_call(
        functools.partial(_diff_prompt_body, tq=tq, ck=ck, lam_init=lam_init),
        grid=(b, n_heads, t // tq),
        in_specs=[_smem_spec(), qspec, kvspec, kvspec, vec, vec, vec, vec,
                  pl.BlockSpec((1, HEAD_DIM), lambda bi, h, qi: (0, 0))],
        out_specs=qspec,
        out_shape=jax.ShapeDtypeStruct((b, t, hd), BF16),
        compiler_params=_params("arbitrary", "arbitrary", "arbitrary"),
        name="diff_attention_prompt",
    )(_alibi_slopes(n_heads), q, k, v, *lams, subln)


def _head_slopes(slope_ref, first, n):
    j = lax.broadcasted_iota(jnp.int32, (n, 1, 1), 0)
    out = jnp.zeros((n, 1, 1), F32)
    for jj in range(n):
        out = jnp.where(j == jj, slope_ref[first + jj] * LOG2E, out)
    return out


def _diff_sample_body(pt_ref, slope_ref, q_ref, kn_ref, vn_ref, lq1, lk1, lq2, lk2, subln_ref, *rest,
                      pps, sub, n_heads, hg, page, t_new, past, lam_init):
    k_pages = rest[:pps]
    v_pages = rest[pps:2 * pps]
    o_ref = rest[2 * pps]
    qcat_ref, bias_ref, srow_ref, m_ref, l_ref, acc_ref = rest[2 * pps + 1:]
    c, n_c = pl.program_id(1), pl.num_programs(1)
    rows = 2 * t_new
    n_groups = n_heads // hg
    r_all, cols = hg * rows, page * hg

    @pl.when(c == 0)
    def _():
        row = lax.broadcasted_iota(jnp.int32, (r_all, cols), 0)
        col = lax.broadcasted_iota(jnp.int32, (r_all, cols), 1)
        row1 = lax.broadcasted_iota(jnp.int32, (r_all, 1), 0)
        same_head = (col % hg) == (row // rows)
        key = (col // hg).astype(F32)
        for gi in range(n_groups):
            heads = range(gi * hg, (gi + 1) * hg)
            qcat_ref[gi] = jnp.concatenate(
                [_diff_qcat(q_ref[0, :, h * HEAD_DIM:(h + 1) * HEAD_DIM]) for h in heads], axis=0)
            srow = jnp.zeros((r_all, 1), F32)
            for j, h in enumerate(heads):
                srow = jnp.where(row1 // rows == j, slope_ref[h] * LOG2E, srow)
            srow_ref[gi] = srow
            bias_ref[gi] = jnp.where(same_head, srow * key, NEG)
        m_ref[...] = jnp.full(m_ref.shape, NEG, F32)
        l_ref[...] = jnp.zeros(l_ref.shape, F32)
        acc_ref[...] = jnp.zeros(acc_ref.shape, F32)

    def page_rows(ref, gi):
        return ref[0, :, gi].reshape(cols, HEAD_DIM).astype(BF16)

    for i0 in range(0, pps, sub):
        pages_now = range(i0, i0 + sub)
        scores = []
        for gi in range(n_groups):
            parts = []
            for i in pages_now:
                base = ((c * pps + i) * page - past).astype(F32)
                parts.append(_nt_dot(qcat_ref[gi], page_rows(k_pages[i], gi)) + bias_ref[gi] + srow_ref[gi] * base)
            scores.append(jnp.concatenate(parts, axis=1))
        for gi in range(n_groups):
            v = jnp.concatenate([page_rows(v_pages[i], gi) for i in pages_now], axis=0)
            m_ref[gi], l_ref[gi], acc_ref[gi] = _softmax_update(
                (m_ref[gi], l_ref[gi], acc_ref[gi]), scores[gi], None, v)

    @pl.when(c == n_c - 1)
    def _():
        lam = _lambda_full(lq1[...], lk1[...], lq2[...], lk2[...], lam_init)
        pad = jnp.zeros((page - t_new, HEAD_DIM), F32)
        r = lax.broadcasted_iota(jnp.int32, (rows, page), 0)
        cc = lax.broadcasted_iota(jnp.int32, (rows, page), 1)
        visible = (jnp.where(r >= t_new, r - t_new, r) >= cc) & (cc < t_new)
        for h in range(n_heads):
            gi, j = divmod(h, hg)
            rs = slice(j * rows, (j + 1) * rows)
            lanes = slice(h * HEAD_DIM, (h + 1) * HEAD_DIM)
            k = jnp.concatenate([kn_ref[0, :, lanes], pad], axis=0).astype(BF16)
            v = jnp.concatenate([vn_ref[0, :, lanes], pad], axis=0).astype(BF16)
            sn = _nt_dot(qcat_ref[gi, rs, :], k) + (slope_ref[h] * LOG2E) * cc.astype(F32)
            carry = _softmax_update((m_ref[gi, rs, :], l_ref[gi, rs, :], acc_ref[gi, rs, :]), sn, visible, v)
            o = _diff_finish(_softmax_finish(carry), lam, lam_init, subln_ref[...])
            o_ref[0, :, lanes] = o.astype(o_ref.dtype)


def diff_attention_sample(q, k_new, v_new, cache_k, cache_v, page_table, lams, subln, lam_init,
                          pages_per_step=8, pages_per_softmax_step=4):
    b, t, hd = q.shape
    n_heads = hd // HEAD_DIM
    n_phys, page = cache_k.shape[:2]
    n_pages = page_table.shape[1]
    hg = math.gcd(n_heads, SUBLANES)
    n_groups = n_heads // hg
    pps = min(pages_per_step, n_pages)
    sub = min(pages_per_softmax_step, pps)
    assert n_pages % pps == 0 and pps % sub == 0 and t <= page
    tok = pl.BlockSpec((1, t, hd), lambda bi, c, pt: (bi, 0, 0))
    vec = pl.BlockSpec((1, DIFF_HALF), lambda bi, c, pt: (0, 0))

    def page_spec(i):
        return pl.BlockSpec((1, page, n_groups, hg, HEAD_DIM), lambda bi, c, pt: (pt[bi, c * pps + i], 0, 0, 0, 0))

    pages = [page_spec(i) for i in range(pps)]
    r_all = hg * 2 * t
    grid_spec = pltpu.PrefetchScalarGridSpec(
        num_scalar_prefetch=1,
        grid=(b, n_pages // pps),
        in_specs=[_smem_spec(), tok, tok, tok, vec, vec, vec, vec,
                  pl.BlockSpec((1, HEAD_DIM), lambda bi, c, pt: (0, 0))] + pages + pages,
        out_specs=tok,
        scratch_shapes=[pltpu.VMEM((n_groups, r_all, HEAD_DIM), BF16),
                        pltpu.VMEM((n_groups, r_all, page * hg), F32),
                        pltpu.VMEM((n_groups, r_all, 1), F32),
                        pltpu.VMEM((n_groups, r_all, 1), F32),
                        pltpu.VMEM((n_groups, r_all, 1), F32),
                        pltpu.VMEM((n_groups, r_all, HEAD_DIM), F32)])
    split = lambda cache: cache.reshape(n_phys, page, n_groups, hg, HEAD_DIM)
    return pl.pallas_call(
        functools.partial(_diff_sample_body, pps=pps, sub=sub, n_heads=n_heads, hg=hg, page=page, t_new=t,
                          past=n_pages * page, lam_init=lam_init),
        grid_spec=grid_spec,
        out_shape=jax.ShapeDtypeStruct((b, t, hd), BF16),
        compiler_params=_params("arbitrary", "arbitrary"),
        name="diff_attention_sample",
    )(page_table, _alibi_slopes(n_heads), q, k_new, v_new, *lams, subln,
      *([split(cache_k)] * pps), *([split(cache_v)] * pps))


def _compress_body(*refs, n_ops, rows_op, nc, nc_pad, interleaved):
    page_refs = refs[:n_ops]
    pos_ref, w1_ref, w2_ref, o_ref, x_ref = refs[n_ops:]
    c, n_c = pl.program_id(1), pl.num_programs(1)
    for i, ref in enumerate(page_refs):
        r0 = pl.multiple_of((c * n_ops + i) * rows_op, rows_op)
        for g in range(NSA_KV):
            if interleaved:
                x_ref[g, pl.ds(r0, rows_op), :] = ref[0, pl.ds(g, rows_op, stride=NSA_KV), :]
            else:
                x_ref[g, pl.ds(r0, rows_op), :] = ref[0, :, g * HEAD_DIM:(g + 1) * HEAD_DIM]

    @pl.when(c == n_c - 1)
    def _():
        w1 = w1_ref[...].astype(BF16)
        w2 = w2_ref[...].astype(BF16)
        for g in range(NSA_KV):
            pieces = [(x_ref[g, pl.ds(j, nc, stride=BLK), :] + pos_ref[j:j + 1, :]).astype(BF16)
                      for j in range(BLK)]
            a = jnp.concatenate(pieces, axis=1)
            mid = _silu(jnp.dot(a, w1, preferred_element_type=F32))
            out = jnp.dot(mid.astype(BF16), w2, preferred_element_type=F32)
            o_ref[0, g, 0:nc, :] = out
            if nc_pad > nc:
                o_ref[0, g, nc:nc_pad, :] = jnp.zeros((nc_pad - nc, HEAD_DIM), F32)


def compress(page_arrays, page_specs, grid, prefetch, length, pos, w1, w2, nc_pad, interleaved):
    n_ops = len(page_arrays)
    rows_op = length // (grid[1] * n_ops)
    nc = length // BLK
    n_pre = len(prefetch)
    const = lambda shape: pl.BlockSpec(shape, lambda bi, c, *pt: (0,) * len(shape))
    grid_spec = pltpu.PrefetchScalarGridSpec(
        num_scalar_prefetch=n_pre, grid=grid,
        in_specs=list(page_specs) + [const(pos.shape), const(w1.shape), const(w2.shape)],
        out_specs=pl.BlockSpec((1, NSA_KV, nc_pad, HEAD_DIM), lambda bi, c, *pt: (bi, 0, 0, 0)),
        scratch_shapes=[pltpu.VMEM((NSA_KV, length, HEAD_DIM), F32)])

    def body(*refs):
        _compress_body(*refs[n_pre:], n_ops=n_ops, rows_op=rows_op, nc=nc, nc_pad=nc_pad, interleaved=interleaved)

    return pl.pallas_call(
        body, grid_spec=grid_spec,
        out_shape=jax.ShapeDtypeStruct((grid[0], NSA_KV, nc_pad, HEAD_DIM), F32),
        compiler_params=_params("arbitrary", "arbitrary"),
        name="compress",
    )(*prefetch, *page_arrays, pos, w1, w2)


def compress_prompt(x, pos, w1, w2, nc_pad):
    b, t, w = x.shape
    spec = pl.BlockSpec((1, t, w), lambda bi, c: (bi, 0, 0))
    return compress([x], [spec], (b, 1), [], t, pos, w1, w2, nc_pad, interleaved=False)


def compress_paged(cache, page_table, pos, w1, w2, nc_pad, pages_per_step=16):
    b, n_pages = page_table.shape
    page = cache.shape[1] // NSA_KV
    pps = min(pages_per_step, n_pages)
    assert n_pages % pps == 0

    def spec(i):
        return pl.BlockSpec((1, page * NSA_KV, HEAD_DIM), lambda bi, c, pt: (pt[bi, c * pps + i], 0, 0))

    return compress([cache] * pps, [spec(i) for i in range(pps)], (b, n_pages // pps), [page_table],
                    n_pages * page, pos, w1, w2, nc_pad, interleaved=True)


def _select_blocks(imp, pos_q, n_blocks):
    t, w = imp.shape
    blk = lax.broadcasted_iota(jnp.int32, (t, w), 1)
    complete = blk * BLK + (BLK - 1) <= pos_q
    cur = blk == pos_q // BLK
    val = jnp.where(cur, BIG_SEL, jnp.where(blk == 0, 0.5 * BIG_SEL, jnp.where(complete, imp, NEG)))
    val = jnp.where(blk < n_blocks, val, -3e38)
    rank = jnp.zeros((t, w), jnp.int32)
    for i in range(n_blocks):
        vi = val[:, i:i + 1]
        ahead = (vi > val) | ((vi == val) & (blk > i))
        rank = rank + ahead.astype(jnp.int32)
    return (rank < min(N_SELECT, n_blocks)) & (blk < n_blocks)


def _expand_blocks(sel, first_block, n_keys):
    w = sel.shape[1]
    blk = lax.broadcasted_iota(jnp.int32, (w, n_keys), 0)
    key = lax.broadcasted_iota(jnp.int32, (w, n_keys), 1)
    onehot = jnp.where(blk == first_block + key // BLK, 1.0, 0.0).astype(BF16)
    return jnp.dot(sel, onehot, preferred_element_type=F32)


def _head_rows(q_ref, g, hpg):
    parts = [q_ref[0, :, (g * hpg + j) * HEAD_DIM:(g * hpg + j + 1) * HEAD_DIM] for j in range(hpg)]
    return (jnp.concatenate(parts, axis=0) * (HEAD_DIM ** -0.5 * LOG2E)).astype(BF16)


def _compressed_branch(q, kcb, vcb, slopes, pos_q, nc, hpg):
    t = pos_q.shape[0]
    w = kcb.shape[0]
    blk_end = lax.broadcasted_iota(jnp.int32, (t, w), 1) * BLK + (BLK - 1)
    dist = pos_q - blk_end
    mask = ((dist >= 0) & (blk_end < nc * BLK))[None]
    s = _nt_dot(q, kcb.astype(BF16)).reshape(hpg, t, w) - slopes * dist.astype(F32)[None]
    s = jnp.where(mask, s, NEG)
    e = jnp.where(mask, jnp.exp2(s - jnp.max(s, axis=-1, keepdims=True)), 0.0)
    p = e / jnp.maximum(jnp.sum(e, axis=-1, keepdims=True), 1e-30)
    o = jnp.dot(p.reshape(hpg * t, w).astype(BF16), vcb.astype(BF16), preferred_element_type=F32)
    return o, jnp.sum(p, axis=0)


def _merge_heads(o_ref, gates, g, hpg, o_cmp, o_slc, o_win):
    t = o_cmp.shape[0] // hpg
    for j in range(hpg):
        head = g * hpg + j
        rows = slice(j * t, (j + 1) * t)
        o = (gates[:, 3 * head:3 * head + 1] * o_cmp[rows]
             + gates[:, 3 * head + 1:3 * head + 2] * o_slc[rows]
             + gates[:, 3 * head + 2:3 * head + 3] * o_win[rows])
        o_ref[0, :, head * HEAD_DIM:(head + 1) * HEAD_DIM] = o.astype(o_ref.dtype)


def _nsa_prompt_body(slope_ref, q_ref, kcb_ref, vcb_ref, ks_ref, vs_ref, kw_ref, vw_ref, gate_ref, o_ref,
                     *, tq, ck, wlen, nc, hpg):
    qi = pl.program_id(1)
    q0 = qi * tq
    pos_q = q0 + lax.broadcasted_iota(jnp.int32, (tq, 1), 0)
    gates = gate_ref[0]
    rel_s = pos_q - lax.broadcasted_iota(jnp.int32, (tq, ck), 1)
    col_s = lax.broadcasted_iota(jnp.int32, (1, 1, ck), 2).astype(F32)
    w0 = pl.multiple_of(jnp.maximum(q0 + tq - wlen, 0), tq)
    dist_w = pos_q - (w0 + lax.broadcasted_iota(jnp.int32, (tq, wlen), 1))
    mask_w = ((dist_w >= 0) & (dist_w <= WINDOW))[None]
    pos_w = (w0 - q0 + lax.broadcasted_iota(jnp.int32, (1, 1, wlen), 2)).astype(F32)
    init = _softmax_init((hpg, tq), hpg * tq, HEAD_DIM)
    for g in range(NSA_KV):
        lanes = pl.ds(g * HEAD_DIM, HEAD_DIM)
        q = _head_rows(q_ref, g, hpg)
        slopes = _head_slopes(slope_ref, g * hpg, hpg)
        o_cmp, imp = _compressed_branch(q, kcb_ref[0, g], vcb_ref[0, g], slopes, pos_q, nc, hpg)
        sel = jnp.where(_select_blocks(imp, pos_q, nc), 1.0, 0.0).astype(BF16)

        def slc_chunk(c, carry, causal):
            k0 = pl.multiple_of(c * ck, ck)
            k = ks_ref[0, pl.ds(k0, ck), lanes].astype(BF16)
            v = vs_ref[0, pl.ds(k0, ck), lanes].astype(BF16)
            mask = _expand_blocks(sel, k0 // BLK, ck) > 0.5
            if causal:
                mask = mask & (rel_s >= k0)
            s = _nt_dot(q, k).reshape(hpg, tq, ck) + slopes * (col_s + (k0 - q0).astype(F32))
            return _softmax_update(carry, s, jnp.broadcast_to(mask[None], s.shape), v, zero_masked=False)

        n_full = q0 // ck
        carry = lax.fori_loop(0, n_full, functools.partial(slc_chunk, causal=False), init)
        carry = lax.fori_loop(n_full, (q0 + tq + ck - 1) // ck, functools.partial(slc_chunk, causal=True), carry)
        o_slc = _softmax_finish(carry)

        kw = kw_ref[0, pl.ds(w0, wlen), lanes].astype(BF16)
        vw = vw_ref[0, pl.ds(w0, wlen), lanes].astype(BF16)
        s = _nt_dot(q, kw).reshape(hpg, tq, wlen) + slopes * pos_w
        o_win = _softmax_finish(_softmax_update(init, s, jnp.broadcast_to(mask_w, s.shape), vw, zero_masked=False))
        _merge_heads(o_ref, gates, g, hpg, o_cmp, o_slc, o_win)


def nsa_attention_prompt(q, kcb, vcb, ks, vs, kw, vw, gates, tq=128, ck=512):
    b, t, hd = q.shape
    n_heads = hd // HEAD_DIM
    hpg = n_heads // NSA_KV
    tq, ck = min(tq, t), min(ck, t)
    wlen = min(WINDOW + tq, t)
    assert ck % tq == 0 and WINDOW % tq == 0 and N_SELECT >= 2
    tok = pl.BlockSpec((1, tq, hd), lambda bi, qi: (bi, qi, 0))
    cmp_spec = pl.BlockSpec((1,) + kcb.shape[1:], lambda bi, qi: (bi, 0, 0, 0))
    seq = pl.BlockSpec((1, t, NSA_KV * HEAD_DIM), lambda bi, qi: (bi, 0, 0))
    return pl.pallas_call(
        functools.partial(_nsa_prompt_body, tq=tq, ck=ck, wlen=wlen, nc=t // BLK, hpg=hpg),
        grid=(b, t // tq),
        in_specs=[_smem_spec(), tok, cmp_spec, cmp_spec, seq, seq, seq, seq,
                  pl.BlockSpec((1, tq, LANES), lambda bi, qi: (bi, qi, 0))],
        out_specs=tok,
        out_shape=jax.ShapeDtypeStruct((b, t, hd), BF16),
        compiler_params=_params("arbitrary", "arbitrary"),
        name="nsa_attention_prompt",
    )(_alibi_slopes(n_heads), q, kcb, vcb, ks, vs, kw, vw, gates)


def _nsa_sample_body(pt_ref, slope_ref, q_ref, kcb_ref, vcb_ref, ksn_ref, vsn_ref, kwn_ref, vwn_ref,
                     wk_ref, wv_ref, gate_ref, *rest, pps, page, t_new, past, hpg, sel_w):
    k_pages = rest[:pps]
    v_pages = rest[pps:2 * pps]
    o_ref = rest[2 * pps]
    q_s, ocmp_s, sel_s, m_s, l_s, acc_s = rest[2 * pps + 1:]
    c, n_c = pl.program_id(1), pl.num_programs(1)
    rows = hpg * t_new
    nc = past // BLK
    n_blocks = -(-(past + t_new) // BLK)
    pos_q = past + lax.broadcasted_iota(jnp.int32, (t_new, 1), 0)
    keys_step = pps * page

    @pl.when(c == 0)
    def _():
        for g in range(NSA_KV):
            q = _head_rows(q_ref, g, hpg)
            q_s[g] = q
            slopes = _head_slopes(slope_ref, g * hpg, hpg)
            o_cmp, imp = _compressed_branch(q, kcb_ref[0, g], vcb_ref[0, g], slopes, pos_q, nc, hpg)
            ocmp_s[g] = o_cmp
            imp = jnp.concatenate([imp, jnp.zeros((t_new, sel_w - imp.shape[1]), F32)], axis=1)
            sel_s[g] = jnp.where(_select_blocks(imp, pos_q, n_blocks), 1.0, 0.0)
        m_s[...] = jnp.full(m_s.shape, NEG, F32)
        l_s[...] = jnp.zeros(l_s.shape, F32)
        acc_s[...] = jnp.zeros(acc_s.shape, F32)

    def group_rows(ref, g):
        return ref[0, pl.ds(g, page, stride=NSA_KV), :]

    rel = pos_q - lax.broadcasted_iota(jnp.int32, (t_new, keys_step), 1)
    for g in range(NSA_KV):
        slopes = _head_slopes(slope_ref, g * hpg, hpg)
        q = q_s[g]
        k0 = c * keys_step
        dist = rel - k0
        mask = (_expand_blocks(sel_s[g].astype(BF16), k0 // BLK, keys_step) > 0.5)[None]
        k = jnp.concatenate([group_rows(k_pages[i], g) for i in range(pps)], axis=0).astype(BF16)
        v = jnp.concatenate([group_rows(v_pages[i], g) for i in range(pps)], axis=0).astype(BF16)
        s = _nt_dot(q, k).reshape(hpg, t_new, keys_step) - slopes * dist.astype(F32)[None]
        carry = _softmax_update((m_s[g], l_s[g], acc_s[g]), s, jnp.broadcast_to(mask, s.shape), v)
        m_s[g], l_s[g], acc_s[g] = carry

    @pl.when(c == n_c - 1)
    def _():
        gates = gate_ref[0]
        pad = jnp.zeros((page - t_new, HEAD_DIM), F32)
        col = lax.broadcasted_iota(jnp.int32, (t_new, page), 1)
        dist_new = (pos_q - past) - col
        for g in range(NSA_KV):
            lanes = pl.ds(g * HEAD_DIM, HEAD_DIM)
            slopes = _head_slopes(slope_ref, g * hpg, hpg)
            q = q_s[g]
            k = jnp.concatenate([ksn_ref[0, :, lanes], pad], axis=0).astype(BF16)
            v = jnp.concatenate([vsn_ref[0, :, lanes], pad], axis=0).astype(BF16)
            sel_new = _expand_blocks(sel_s[g].astype(BF16), nc, page) > 0.5
            mask = (sel_new & (dist_new >= 0) & (col < t_new))[None]
            s = _nt_dot(q, k).reshape(hpg, t_new, page) - slopes * dist_new.astype(F32)[None]
            carry = _softmax_update((m_s[g], l_s[g], acc_s[g]), s, jnp.broadcast_to(mask, s.shape), v)
            o_slc = _softmax_finish(carry)
            w_buf = wk_ref.shape[1]
            colw = lax.broadcasted_iota(jnp.int32, (t_new, w_buf), 1)
            dist_w = pos_q - (past - w_buf + colw)
            carry = _softmax_init((hpg, t_new), rows, HEAD_DIM)
            sw = _nt_dot(q, wk_ref[0, :, lanes].astype(BF16)).reshape(hpg, t_new, w_buf)
            sw = sw - slopes * dist_w.astype(F32)[None]
            mask_w = ((dist_w >= 0) & (dist_w <= WINDOW))[None]
            carry = _softmax_update(carry, sw, jnp.broadcast_to(mask_w, sw.shape),
                                    wv_ref[0, :, lanes].astype(BF16))
            k = jnp.concatenate([kwn_ref[0, :, lanes], pad], axis=0).astype(BF16)
            v = jnp.concatenate([vwn_ref[0, :, lanes], pad], axis=0).astype(BF16)
            mask_n = ((dist_new >= 0) & (dist_new <= WINDOW) & (col < t_new))[None]
            sn = _nt_dot(q, k).reshape(hpg, t_new, page) - slopes * dist_new.astype(F32)[None]
            carry = _softmax_update(carry, sn, jnp.broadcast_to(mask_n, sn.shape), v)
            o_win = _softmax_finish(carry)
            _merge_heads(o_ref, gates, g, hpg, ocmp_s[g], o_slc, o_win)


def nsa_attention_sample(q, kcb, vcb, ks_new, vs_new, kw_new, vw_new, win_k, win_v, gates,
                         cache_k, cache_v, page_table, pages_per_step=8):
    b, t, hd = q.shape
    n_heads = hd // HEAD_DIM
    hpg = n_heads // NSA_KV
    n_pages = page_table.shape[1]
    page = cache_k.shape[1] // NSA_KV
    past = n_pages * page
    pps = min(pages_per_step, n_pages)
    n_blocks = -(-(past + t) // BLK)
    sel_w = -(-n_blocks // LANES) * LANES
    assert n_pages % pps == 0 and t <= page and page % BLK == 0 and kcb.shape[2] * BLK >= past
    width = NSA_KV * HEAD_DIM
    tok = pl.BlockSpec((1, t, hd), lambda bi, c, pt: (bi, 0, 0))
    new = pl.BlockSpec((1, t, width), lambda bi, c, pt: (bi, 0, 0))
    cmp_spec = pl.BlockSpec((1,) + kcb.shape[1:], lambda bi, c, pt: (bi, 0, 0, 0))
    win = pl.BlockSpec((1,) + win_k.shape[1:], lambda bi, c, pt: (bi, 0, 0))

    def page_spec(i):
        return pl.BlockSpec((1, page * NSA_KV, HEAD_DIM), lambda bi, c, pt: (pt[bi, c * pps + i], 0, 0))

    pages = [page_spec(i) for i in range(pps)]
    rows = hpg * t
    grid_spec = pltpu.PrefetchScalarGridSpec(
        num_scalar_prefetch=1,
        grid=(b, n_pages // pps),
        in_specs=[_smem_spec(), tok, cmp_spec, cmp_spec, new, new, new, new, win, win,
                  pl.BlockSpec((1, t, LANES), lambda bi, c, pt: (bi, 0, 0))] + pages + pages,
        out_specs=tok,
        scratch_shapes=[pltpu.VMEM((NSA_KV, rows, HEAD_DIM), BF16),
                        pltpu.VMEM((NSA_KV, rows, HEAD_DIM), F32),
                        pltpu.VMEM((NSA_KV, t, sel_w), F32),
                        pltpu.VMEM((NSA_KV, hpg, t, 1), F32),
                        pltpu.VMEM((NSA_KV, hpg, t, 1), F32),
                        pltpu.VMEM((NSA_KV, rows, HEAD_DIM), F32)])
    return pl.pallas_call(
        functools.partial(_nsa_sample_body, pps=pps, page=page, t_new=t, past=past, hpg=hpg, sel_w=sel_w),
        grid_spec=grid_spec,
        out_shape=jax.ShapeDtypeStruct((b, t, hd), BF16),
        compiler_params=_params("arbitrary", "arbitrary"),
        name="nsa_attention_sample",
    )(page_table, _alibi_slopes(n_heads), q, kcb, vcb, ks_new, vs_new, kw_new, vw_new, win_k, win_v, gates,
      *([cache_k] * pps), *([cache_v] * pps))


def _last_rows(a, n):
    a = jnp.pad(a, ((0, 0), (n, 0)) + ((0, 0),) * (a.ndim - 2))
    return a[:, a.shape[1] - n:]


def _project(x, mods, norm_g, w_in_t, mix_diff, mix_nsa):
    b, t, d = x.shape
    h = modulate(x, mods[0], mods[1], norm_g)
    seg = lambda col0, n: matmul(h, w_in_t, col0=col0, ncols=n, wt=True).reshape(b, t, n)
    dq, dk, dv = seg(0, mix_diff), seg(mix_diff, mix_diff), seg(2 * mix_diff, mix_diff)
    nq = seg(3 * mix_diff, mix_nsa)
    n_gate = 3 * (mix_nsa // HEAD_DIM)
    outs = kv_gate_projection(h, w_in_t, 3 * mix_diff + mix_nsa, 6, n_gate)
    kvs = [o.reshape(b, t, NSA_KV * HEAD_DIM) for o in outs[:6]]
    gates = outs[6].reshape(b, t, LANES)
    return dq, dk, dv, nq, kvs, gates


def _finish(x, o_diff, o_nsa, mods, norm2_g, w_out, w_up, w_down, final_g):
    b, t, d = x.shape
    mix_in = jnp.concatenate([o_diff, o_nsa], axis=-1).reshape(b * t, -1)
    mix = matmul(mix_in, w_out)
    x1, h2 = residual_modulate(x, mix, mods[2], mods[3], mods[4], norm2_g)
    u = matmul(h2, w_up, out_dtype=BF16, relu2=True)
    ffn = matmul(u, w_down, tn=1024, tk=2048)
    return residual_final_norm(x1, ffn, mods[5], final_g)


def kernel(x_prompt, x_sample, c_prompt, c_sample, cache_diff_k, cache_diff_v, cache_cmp_k, cache_cmp_v, cache_slc_k, cache_slc_v, state_win_k, state_win_v, page_table, norm1_g, norm2_g, ada_w, ada_b, w_in, w_out, diff_lq1, diff_lk1, diff_lq2, diff_lk2, diff_subln_g, cmp_k_pos, cmp_k_w1, cmp_k_w2, cmp_v_pos, cmp_v_w1, cmp_v_w2, w_up, w_down, final_g):
    depth = ada_w.shape[0]
    assert depth == 1
    l = 0
    bp, tp, d = x_prompt.shape
    bs, ts, _ = x_sample.shape
    n_phys, page, n_heads_diff, _ = cache_diff_k.shape[1:]
    mix_diff = n_heads_diff * HEAD_DIM
    mix_nsa = d - mix_diff
    w_buf = state_win_k.shape[2]
    width = NSA_KV * HEAD_DIM
    lam_init = 0.8 - 0.6 * math.exp(-0.3 * l)
    lams = [a[l].reshape(1, DIFF_HALF) for a in (diff_lq1, diff_lk1, diff_lq2, diff_lk2)]
    subln = diff_subln_g[l].reshape(1, HEAD_DIM)

    n_c = bp + bs
    n_c_pad = -(-n_c // 16) * 16
    c_all = jnp.concatenate([c_prompt, c_sample, jnp.zeros((n_c_pad - n_c, d), F32)], axis=0)
    mod = ada_modulation(c_all, ada_w[l], ada_b[l]).reshape(n_c_pad, 6, 1, d)
    mods_p = [mod[:bp, i] for i in range(6)]
    mods_s = [mod[bp:n_c, i] for i in range(6)]

    w_in_t = jnp.swapaxes(w_in[l], 0, 1)

    dq, dk_p, dv_p, nq, kvs_p, gates = _project(x_prompt, mods_p, norm1_g[l], w_in_t, mix_diff, mix_nsa)
    o_diff = diff_attention_prompt(dq, dk_p, dv_p, lams, subln, lam_init)
    nc_pad = -(-max(tp, page_table.shape[1] * page) // (BLK * LANES)) * LANES
    kcb = compress_prompt(kvs_p[0], cmp_k_pos[l], cmp_k_w1[l], cmp_k_w2[l], nc_pad)
    vcb = compress_prompt(kvs_p[1], cmp_v_pos[l], cmp_v_w1[l], cmp_v_w2[l], nc_pad)
    o_nsa = nsa_attention_prompt(nq, kcb, vcb, kvs_p[2], kvs_p[3], kvs_p[4], kvs_p[5], gates)
    y_prompt = _finish(x_prompt, o_diff, o_nsa, mods_p, norm2_g[l], w_out[l], w_up[l], w_down[l], final_g)

    dq, dk_s, dv_s, nq, kvs_s, gates = _project(x_sample, mods_s, norm1_g[l], w_in_t, mix_diff, mix_nsa)
    o_diff = diff_attention_sample(dq, dk_s, dv_s, cache_diff_k[l], cache_diff_v[l], page_table, lams, subln, lam_init)
    flat_nsa = lambda c: c[l].reshape(n_phys, page * NSA_KV, HEAD_DIM)
    kcb = compress_paged(flat_nsa(cache_cmp_k), page_table, cmp_k_pos[l], cmp_k_w1[l], cmp_k_w2[l], nc_pad)
    vcb = compress_paged(flat_nsa(cache_cmp_v), page_table, cmp_v_pos[l], cmp_v_w1[l], cmp_v_w2[l], nc_pad)
    win_k = state_win_k[l].reshape(bs, w_buf, width)
    win_v = state_win_v[l].reshape(bs, w_buf, width)
    o_nsa = nsa_attention_sample(nq, kcb, vcb, kvs_s[2], kvs_s[3], kvs_s[4], kvs_s[5], win_k, win_v, gates,
                                 flat_nsa(cache_slc_k), flat_nsa(cache_slc_v), page_table)
    y_sample = _finish(x_sample, o_diff, o_nsa, mods_s, norm2_g[l], w_out[l], w_up[l], w_down[l], final_g)

    heads = lambda a, b, t, h: a.reshape(1, b, t, h, HEAD_DIM)
    out_p = [heads(dk_p, bp, tp, n_heads_diff), heads(dv_p, bp, tp, n_heads_diff)]
    out_p += [heads(a, bp, tp, NSA_KV) for a in kvs_p[:4]]
    out_p += [heads(_last_rows(a, w_buf), bp, w_buf, NSA_KV) for a in kvs_p[4:]]
    out_s = [heads(dk_s, bs, ts, n_heads_diff), heads(dv_s, bs, ts, n_heads_diff)]
    out_s += [heads(a, bs, ts, NSA_KV) for a in kvs_s[:4]]
    out_s += [heads(jnp.concatenate([w, a], axis=1)[:, ts:], bs, w_buf, NSA_KV)
              for w, a in ((win_k, kvs_s[4]), (win_v, kvs_s[5]))]
    return (y_prompt, y_sample, *out_p, *out_s)
```

```python
import functools
import math

import numpy as np
import jax
import jax.numpy as jnp
from jax import lax
from jax.experimental import pallas as pl
from jax.experimental.pallas import tpu as pltpu

F32 = jnp.float32
BF16 = jnp.bfloat16

DIFF_HALF = 64
HEAD_DIM = 128
NSA_KV = 2
BLK = 64
N_SELECT = 16
WINDOW = 512
EPS = 1e-6
NEG = -1e30
BIG_SEL = 1e4
LOG2E = math.log2(math.e)
LANES = 128
SUBLANES = 8
BLK_PITCH = BLK + SUBLANES
VMEM_LIMIT_BYTES = 56 * 1024 * 1024


def _params(*sem):
    return pltpu.CompilerParams(dimension_semantics=sem, vmem_limit_bytes=VMEM_LIMIT_BYTES)


def _alibi_slopes(n):
    return jnp.asarray(np.exp2(-8.0 * np.arange(1, n + 1) / n), dtype=F32)


def _silu(x):
    return x * jax.nn.sigmoid(x)


def _smem_spec():
    return pl.BlockSpec(memory_space=pltpu.SMEM)


def _ada_body(c_ref, w_ref, b_ref, o_ref):
    a = _silu(c_ref[...]).astype(BF16)
    o_ref[...] = jnp.dot(a, w_ref[...].astype(BF16), preferred_element_type=F32) + b_ref[...]


def ada_modulation(c, w, b, tn=512):
    m, k = c.shape
    n = w.shape[1]
    tn = min(tn, n)
    return pl.pallas_call(
        _ada_body,
        grid=(n // tn,),
        in_specs=[pl.BlockSpec((m, k), lambda j: (0, 0)),
                  pl.BlockSpec((k, tn), lambda j: (0, j)),
                  pl.BlockSpec((1, tn), lambda j: (0, j))],
        out_specs=pl.BlockSpec((m, tn), lambda j: (0, j)),
        out_shape=jax.ShapeDtypeStruct((m, n), F32),
        compiler_params=_params("arbitrary"),
        name="ada_modulation",
    )(c, w, b.reshape(1, n))


def _rms(x, g):
    return x * lax.rsqrt(jnp.mean(x * x, axis=-1, keepdims=True) + EPS) * g


def _modulate_body(x_ref, sh_ref, sc_ref, g_ref, h_ref):
    h = _rms(x_ref[...], g_ref[...]) * (1.0 + sc_ref[...]) + sh_ref[...]
    h_ref[...] = h.reshape(h_ref.shape).astype(h_ref.dtype)


def _mid_body(x_ref, mix_ref, g1_ref, sh_ref, sc_ref, g_ref, x1_ref, h_ref):
    x1 = x_ref[...] + g1_ref[...] * mix_ref[...].reshape(x_ref.shape)
    x1_ref[...] = x1
    h = _rms(x1, g_ref[...]) * (1.0 + sc_ref[...]) + sh_ref[...]
    h_ref[...] = h.reshape(h_ref.shape).astype(h_ref.dtype)


def _final_body(x_ref, ffn_ref, g2_ref, g_ref, y_ref):
    x2 = x_ref[...] + g2_ref[...] * ffn_ref[...].reshape(x_ref.shape)
    y_ref[...] = _rms(x2, g_ref[...])


def _row_tiles(b, t):
    if t >= 256:
        return 1, 256
    return b, t


def _row_specs(b, t, d):
    bb, tt = _row_tiles(b, t)
    nt = t // tt
    grid = (b // bb, nt)
    x3 = pl.BlockSpec((bb, tt, d), lambda i, j: (i, j, 0))
    mod = pl.BlockSpec((bb, 1, d), lambda i, j: (i, 0, 0))
    gain = pl.BlockSpec((1, 1, d), lambda i, j: (0, 0, 0))
    flat = pl.BlockSpec((bb * tt, d), lambda i, j: (i * nt + j, 0))
    return grid, x3, mod, gain, flat


def modulate(x, shift, scale, g):
    b, t, d = x.shape
    grid, x3, mod, gain, flat = _row_specs(b, t, d)
    return pl.pallas_call(
        _modulate_body, grid=grid,
        in_specs=[x3, mod, mod, gain], out_specs=flat,
        out_shape=jax.ShapeDtypeStruct((b * t, d), BF16),
        compiler_params=_params("arbitrary", "arbitrary"),
        name="modulate",
    )(x, shift, scale, g.reshape(1, 1, d))


def residual_modulate(x, mix, gate1, shift, scale, g):
    b, t, d = x.shape
    grid, x3, mod, gain, flat = _row_specs(b, t, d)
    return pl.pallas_call(
        _mid_body, grid=grid,
        in_specs=[x3, flat, mod, mod, mod, gain], out_specs=[x3, flat],
        out_shape=[jax.ShapeDtypeStruct((b, t, d), F32), jax.ShapeDtypeStruct((b * t, d), BF16)],
        compiler_params=_params("arbitrary", "arbitrary"),
        name="residual_modulate",
    )(x, mix, gate1, shift, scale, g.reshape(1, 1, d))


def residual_final_norm(x, ffn, gate2, g):
    b, t, d = x.shape
    grid, x3, mod, gain, flat = _row_specs(b, t, d)
    return pl.pallas_call(
        _final_body, grid=grid,
        in_specs=[x3, flat, mod, gain], out_specs=x3,
        out_shape=jax.ShapeDtypeStruct((b, t, d), F32),
        compiler_params=_params("arbitrary", "arbitrary"),
        name="residual_final_norm",
    )(x, ffn, gate2, g.reshape(1, 1, d))


def _nt_dot(a, b):
    return lax.dot_general(a, b, (((1,), (1,)), ((), ())), preferred_element_type=F32)


def _w_dot(a, w, w_rows_are_outputs):
    w = w.astype(BF16)
    return _nt_dot(a, w) if w_rows_are_outputs else jnp.dot(a, w, preferred_element_type=F32)


def _mm_body(a_ref, w_ref, o_ref, *scratch, nk, relu2, wt):
    part = _w_dot(a_ref[...], w_ref[...], wt)

    def finish(r):
        if relu2:
            r = jnp.square(jnp.maximum(r, 0.0))
        o_ref[...] = r.astype(o_ref.dtype)

    if nk == 1:
        finish(part)
    else:
        acc_ref, = scratch
        k = pl.program_id(2)

        @pl.when(k == 0)
        def _():
            acc_ref[...] = part

        @pl.when(k > 0)
        def _():
            acc_ref[...] += part

        @pl.when(k == nk - 1)
        def _():
            finish(acc_ref[...])


def matmul(a, w, *, col0=0, ncols=None, tm=1024, tn=512, tk=None, out_dtype=F32, relu2=False, wt=False):
    m, k = a.shape
    n_total = w.shape[0] if wt else w.shape[1]
    ncols = n_total - col0 if ncols is None else ncols
    tm, tn = min(tm, m), min(tn, ncols)
    tk = k if tk is None else min(tk, k)
    assert m % tm == 0 and ncols % tn == 0 and k % tk == 0 and col0 % tn == 0
    nk, jb = k // tk, col0 // tn
    if wt:
        w_spec = pl.BlockSpec((tn, tk), lambda i, j, kk: (jb + j, kk))
    else:
        w_spec = pl.BlockSpec((tk, tn), lambda i, j, kk: (kk, jb + j))
    return pl.pallas_call(
        functools.partial(_mm_body, nk=nk, relu2=relu2, wt=wt),
        grid=(m // tm, ncols // tn, nk),
        in_specs=[pl.BlockSpec((tm, tk), lambda i, j, kk: (i, kk)), w_spec],
        out_specs=pl.BlockSpec((tm, tn), lambda i, j, kk: (i, j)),
        out_shape=jax.ShapeDtypeStruct((m, ncols), out_dtype),
        scratch_shapes=[pltpu.VMEM((tm, tn), F32)] if nk > 1 else [],
        compiler_params=_params("arbitrary", "arbitrary", "arbitrary"),
        name="matmul",
    )(a, w)


def _kv_gate_body(a_ref, w_ref, *o_refs, n_kv, n_gate):
    j = pl.program_id(1)
    r = _w_dot(a_ref[...], w_ref[...], True)
    for n in range(n_kv):
        @pl.when(j == n)
        def _(n=n):
            o_refs[n][...] = r

    @pl.when(j == n_kv)
    def _():
        g = r[:, :LANES]
        col = lax.broadcasted_iota(jnp.int32, g.shape, 1)
        o_refs[n_kv][...] = jnp.where(col < n_gate, jax.nn.sigmoid(g), 0.0)


def kv_gate_projection(a, w, col0, n_kv, n_gate, tm=1024):
    m, k = a.shape
    tm = min(tm, m)
    width = 2 * LANES
    assert col0 % width == 0 and n_gate <= LANES
    jb = col0 // width
    kv = pl.BlockSpec((tm, width), lambda i, j: (i, 0))
    return pl.pallas_call(
        functools.partial(_kv_gate_body, n_kv=n_kv, n_gate=n_gate),
        grid=(m // tm, n_kv + 1),
        in_specs=[pl.BlockSpec((tm, k), lambda i, j: (i, 0)),
                  pl.BlockSpec((width, k), lambda i, j: (jb + j, 0))],
        out_specs=[kv] * n_kv + [pl.BlockSpec((tm, LANES), lambda i, j: (i, 0))],
        out_shape=[jax.ShapeDtypeStruct((m, 2 * LANES), F32)] * n_kv + [jax.ShapeDtypeStruct((m, LANES), F32)],
        compiler_params=_params("arbitrary", "arbitrary"),
        name="kv_gate_projection",
    )(a, w)


def _softmax_update(carry, s, mask, v, zero_masked=True):
    m, l, acc = carry
    if mask is not None:
        s = jnp.where(mask, s, NEG)
    m_new = jnp.maximum(m, jnp.max(s, axis=-1, keepdims=True))
    p = jnp.exp2(s - m_new)
    if mask is not None and zero_masked:
        p = jnp.where(mask, p, 0.0)
    alpha = jnp.exp2(m - m_new)
    l = alpha * l + jnp.sum(p, axis=-1, keepdims=True)
    rows = acc.shape[0]
    pv = jnp.dot(p.reshape(rows, p.shape[-1]).astype(BF16), v, preferred_element_type=F32)
    acc = alpha.reshape(rows, 1) * acc + pv
    return m_new, l, acc


def _softmax_init(lead, rows_total, d):
    return (jnp.full(lead + (1,), NEG, F32), jnp.zeros(lead + (1,), F32), jnp.zeros((rows_total, d), F32))


def _softmax_finish(carry):
    _, l, acc = carry
    return acc / jnp.maximum(l.reshape(acc.shape[0], 1), 1e-30)


def _lambda_full(lq1, lk1, lq2, lk2, lam_init):
    a = jnp.sum(lq1 * lk1, axis=-1, keepdims=True)
    b = jnp.sum(lq2 * lk2, axis=-1, keepdims=True)
    return jnp.exp(a) - jnp.exp(b) + lam_init


def _diff_qcat(q):
    lane = lax.broadcasted_iota(jnp.int32, q.shape, 1)
    qs = q * (DIFF_HALF ** -0.5 * LOG2E)
    first = jnp.where(lane < DIFF_HALF, qs, 0.0)
    second = jnp.where(lane >= DIFF_HALF, qs, 0.0)
    return jnp.concatenate([first, second], axis=0).astype(BF16)


def _diff_finish(o12, lam, lam_init, subln):
    t = o12.shape[0] // 2
    o = o12[:t] - lam * o12[t:]
    return _rms(o, subln) * (1.0 - lam_init)


def _diff_prompt_body(slope_ref, q_ref, k_ref, v_ref, lq1, lk1, lq2, lk2, subln_ref, o_ref, *, tq, ck, lam_init):
    h, qi = pl.program_id(1), pl.program_id(2)
    slope = slope_ref[h] * LOG2E
    qcat = _diff_qcat(q_ref[0])
    q0 = qi * tq
    row = lax.broadcasted_iota(jnp.int32, (2 * tq, 1), 0)
    pos_q = q0 + jnp.where(row >= tq, row - tq, row)
    col = lax.broadcasted_iota(jnp.int32, (1, ck), 1)
    colf = col.astype(F32)

    def chunk(c, carry, causal):
        k0 = pl.multiple_of(c * ck, ck)
        k = k_ref[0, pl.ds(k0, ck), :].astype(BF16)
        v = v_ref[0, pl.ds(k0, ck), :].astype(BF16)
        s = _nt_dot(qcat, k) + slope * (colf + (k0 - q0).astype(F32))
        return _softmax_update(carry, s, (pos_q >= k0 + col) if causal else None, v, zero_masked=False)

    n_full = q0 // ck
    n_chunks = (q0 + tq + ck - 1) // ck
    carry = _softmax_init((2 * tq,), 2 * tq, HEAD_DIM)
    carry = lax.fori_loop(0, n_full, functools.partial(chunk, causal=False), carry)
    carry = lax.fori_loop(n_full, n_chunks, functools.partial(chunk, causal=True), carry)
    lam = _lambda_full(lq1[...], lk1[...], lq2[...], lk2[...], lam_init)
    o_ref[0] = _diff_finish(_softmax_finish(carry), lam, lam_init, subln_ref[...]).astype(o_ref.dtype)


def diff_attention_prompt(q, k, v, lams, subln, lam_init, tq=256, ck=512):
    b, t, hd = q.shape
    n_heads = hd // HEAD_DIM
    tq, ck = min(tq, t), min(ck, t)
    assert ck % tq == 0
    qspec = pl.BlockSpec((1, tq, HEAD_DIM), lambda bi, h, qi: (bi, qi, h))
    kvspec = pl.BlockSpec((1, t, HEAD_DIM), lambda bi, h, qi: (bi, 0, h))
    vec = pl.BlockSpec((1, DIFF_HALF), lambda bi, h, qi: (0, 0))
    return pl.pallas_call(
        functools.partial(_diff_prompt_body, tq=tq, ck=ck, lam_init=lam_init),
        grid=(b, n_heads, t // tq),
        in_specs=[_smem_spec(), qspec, kvspec, kvspec, vec, vec, vec, vec,
                  pl.BlockSpec((1, HEAD_DIM), lambda bi, h, qi: (0, 0))],
        out_specs=qspec,
        out_shape=jax.ShapeDtypeStruct((b, t, hd), BF16),
        compiler_params=_params("arbitrary", "arbitrary", "arbitrary"),
        name="diff_attention_prompt",
    )(_alibi_slopes(n_heads), q, k, v, *lams, subln)


def _head_slopes(slope_ref, first, n):
    j = lax.broadcasted_iota(jnp.int32, (n, 1, 1), 0)
    out = jnp.zeros((n, 1, 1), F32)
    for jj in range(n):
        out = jnp.where(j == jj, slope_ref[first + jj] * LOG2E, out)
    return out


def _diff_sample_body(pt_ref, slope_ref, q_ref, kn_ref, vn_ref, lq1, lk1, lq2, lk2, subln_ref, *rest,
                      pps, sub, n_heads, hg, page, t_new, past, lam_init):
    k_pages = rest[:pps]
    v_pages = rest[pps:2 * pps]
    o_ref = rest[2 * pps]
    qcat_ref, bias_ref, srow_ref, m_ref, l_ref, acc_ref = rest[2 * pps + 1:]
    c, n_c = pl.program_id(1), pl.num_programs(1)
    rows = 2 * t_new
    n_groups = n_heads // hg
    r_all, cols = hg * rows, page * hg

    @pl.when(c == 0)
    def _():
        row = lax.broadcasted_iota(jnp.int32, (r_all, cols), 0)
        col = lax.broadcasted_iota(jnp.int32, (r_all, cols), 1)
        row1 = lax.broadcasted_iota(jnp.int32, (r_all, 1), 0)
        same_head = (col % hg) == (row // rows)
        key = (col // hg).astype(F32)
        for gi in range(n_groups):
            heads = range(gi * hg, (gi + 1) * hg)
            qcat_ref[gi] = jnp.concatenate(
                [_diff_qcat(q_ref[0, :, h * HEAD_DIM:(h + 1) * HEAD_DIM]) for h in heads], axis=0)
            srow = jnp.zeros((r_all, 1), F32)
            for j, h in enumerate(heads):
                srow = jnp.where(row1 // rows == j, slope_ref[h] * LOG2E, srow)
            srow_ref[gi] = srow
            bias_ref[gi] = jnp.where(same_head, srow * key, NEG)
        m_ref[...] = jnp.full(m_ref.shape, NEG, F32)
        l_ref[...] = jnp.zeros(l_ref.shape, F32)
        acc_ref[...] = jnp.zeros(acc_ref.shape, F32)

    def page_rows(ref, gi):
        return ref[0, :, gi].reshape(cols, HEAD_DIM).astype(BF16)

    for i0 in range(0, pps, sub):
        pages_now = range(i0, i0 + sub)
        scores = []
        for gi in range(n_groups):
            parts = []
            for i in pages_now:
                base = ((c * pps + i) * page - past).astype(F32)
                parts.append(_nt_dot(qcat_ref[gi], page_rows(k_pages[i], gi)) + bias_ref[gi] + srow_ref[gi] * base)
            scores.append(jnp.concatenate(parts, axis=1))
        for gi in range(n_groups):
            v = jnp.concatenate([page_rows(v_pages[i], gi) for i in pages_now], axis=0)
            m_ref[gi], l_ref[gi], acc_ref[gi] = _softmax_update(
                (m_ref[gi], l_ref[gi], acc_ref[gi]), scores[gi], None, v)

    @pl.when(c == n_c - 1)
    def _():
        lam = _lambda_full(lq1[...], lk1[...], lq2[...], lk2[...], lam_init)
        pad = jnp.zeros((page - t_new, HEAD_DIM), F32)
        r = lax.broadcasted_iota(jnp.int32, (rows, page), 0)
        cc = lax.broadcasted_iota(jnp.int32, (rows, page), 1)
        visible = (jnp.where(r >= t_new, r - t_new, r) >= cc) & (cc < t_new)
        for h in range(n_heads):
            gi, j = divmod(h, hg)
            rs = slice(j * rows, (j + 1) * rows)
            lanes = slice(h * HEAD_DIM, (h + 1) * HEAD_DIM)
            k = jnp.concatenate([kn_ref[0, :, lanes], pad], axis=0).astype(BF16)
            v = jnp.concatenate([vn_ref[0, :, lanes], pad], axis=0).astype(BF16)
            sn = _nt_dot(qcat_ref[gi, rs, :], k) + (slope_ref[h] * LOG2E) * cc.astype(F32)
            carry = _softmax_update((m_ref[gi, rs, :], l_ref[gi, rs, :], acc_ref[gi, rs, :]), sn, visible, v)
            o = _diff_finish(_softmax_finish(carry), lam, lam_init, subln_ref[...])
            o_ref[0, :, lanes] = o.astype(o_ref.dtype)


def diff_attention_sample(q, k_new, v_new, cache_k, cache_v, page_table, lams, subln, lam_init,
                          pages_per_step=8, pages_per_softmax_step=4):
    b, t, hd = q.shape
    n_heads = hd // HEAD_DIM
    n_phys, page = cache_k.shape[:2]
    n_pages = page_table.shape[1]
    hg = math.gcd(n_heads, SUBLANES)
    n_groups = n_heads // hg
    pps = min(pages_per_step, n_pages)
    sub = min(pages_per_softmax_step, pps)
    assert n_pages % pps == 0 and pps % sub == 0 and t <= page
    tok = pl.BlockSpec((1, t, hd), lambda bi, c, pt: (bi, 0, 0))
    vec = pl.BlockSpec((1, DIFF_HALF), lambda bi, c, pt: (0, 0))

    def page_spec(i):
        return pl.BlockSpec((1, page, n_groups, hg, HEAD_DIM), lambda bi, c, pt: (pt[bi, c * pps + i], 0, 0, 0, 0))

    pages = [page_spec(i) for i in range(pps)]
    r_all = hg * 2 * t
    grid_spec = pltpu.PrefetchScalarGridSpec(
        num_scalar_prefetch=1,
        grid=(b, n_pages // pps),
        in_specs=[_smem_spec(), tok, tok, tok, vec, vec, vec, vec,
                  pl.BlockSpec((1, HEAD_DIM), lambda bi, c, pt: (0, 0))] + pages + pages,
        out_specs=tok,
        scratch_shapes=[pltpu.VMEM((n_groups, r_all, HEAD_DIM), BF16),
                        pltpu.VMEM((n_groups, r_all, page * hg), F32),
                        pltpu.VMEM((n_groups, r_all, 1), F32),
                        pltpu.VMEM((n_groups, r_all, 1), F32),
                        pltpu.VMEM((n_groups, r_all, 1), F32),
                        pltpu.VMEM((n_groups, r_all, HEAD_DIM), F32)])
    split = lambda cache: cache.reshape(n_phys, page, n_groups, hg, HEAD_DIM)
    return pl.pallas_call(
        functools.partial(_diff_sample_body, pps=pps, sub=sub, n_heads=n_heads, hg=hg, page=page, t_new=t,
                          past=n_pages * page, lam_init=lam_init),
        grid_spec=grid_spec,
        out_shape=jax.ShapeDtypeStruct((b, t, hd), BF16),
        compiler_params=_params("arbitrary", "arbitrary"),
        name="diff_attention_sample",
    )(page_table, _alibi_slopes(n_heads), q, k_new, v_new, *lams, subln,
      *([split(cache_k)] * pps), *([split(cache_v)] * pps))


def _compress_body(*refs, n_ops, rows_op, nc, nc_pad, interleaved):
    page_refs = refs[:n_ops]
    pos_ref, w1_ref, w2_ref, o_ref, x_ref = refs[n_ops:]
    c, n_c = pl.program_id(1), pl.num_programs(1)
    blocks_op = rows_op // BLK
    for i, ref in enumerate(page_refs):
        for bi in range(blocks_op):
            blk = (c * n_ops + i) * blocks_op + bi
            r0 = pl.multiple_of(blk * BLK_PITCH, SUBLANES)
            for g in range(NSA_KV):
                if interleaved:
                    rows = ref[0, pl.ds(bi * BLK * NSA_KV + g, BLK, stride=NSA_KV), :]
                else:
                    rows = ref[0, bi * BLK:(bi + 1) * BLK, g * HEAD_DIM:(g + 1) * HEAD_DIM]
                x_ref[g, pl.ds(r0, BLK), :] = rows

    @pl.when(c == n_c - 1)
    def _():
        w1 = w1_ref[...].astype(BF16)
        w2 = w2_ref[...].astype(BF16)
        for g in range(NSA_KV):
            pieces = [(x_ref[g, pl.ds(j, nc, stride=BLK_PITCH), :] + pos_ref[j:j + 1, :]).astype(BF16)
                      for j in range(BLK)]
            a = jnp.concatenate(pieces, axis=1)
            mid = _silu(jnp.dot(a, w1, preferred_element_type=F32))
            out = jnp.dot(mid.astype(BF16), w2, preferred_element_type=F32)
            o_ref[0, g, 0:nc, :] = out
            if nc_pad > nc:
                o_ref[0, g, nc:nc_pad, :] = jnp.zeros((nc_pad - nc, HEAD_DIM), F32)


def compress(page_arrays, page_specs, grid, prefetch, length, pos, w1, w2, nc_pad, interleaved):
    n_ops = len(page_arrays)
    rows_op = length // (grid[1] * n_ops)
    nc = length // BLK
    n_pre = len(prefetch)
    const = lambda shape: pl.BlockSpec(shape, lambda bi, c, *pt: (0,) * len(shape))
    grid_spec = pltpu.PrefetchScalarGridSpec(
        num_scalar_prefetch=n_pre, grid=grid,
        in_specs=list(page_specs) + [const(pos.shape), const(w1.shape), const(w2.shape)],
        out_specs=pl.BlockSpec((1, NSA_KV, nc_pad, HEAD_DIM), lambda bi, c, *pt: (bi, 0, 0, 0)),
        scratch_shapes=[pltpu.VMEM((NSA_KV, nc * BLK_PITCH, HEAD_DIM), F32)])

    def body(*refs):
        _compress_body(*refs[n_pre:], n_ops=n_ops, rows_op=rows_op, nc=nc, nc_pad=nc_pad, interleaved=interleaved)

    return pl.pallas_call(
        body, grid_spec=grid_spec,
        out_shape=jax.ShapeDtypeStruct((grid[0], NSA_KV, nc_pad, HEAD_DIM), F32),
        compiler_params=_params("arbitrary", "arbitrary"),
        name="compress",
    )(*prefetch, *page_arrays, pos, w1, w2)


def compress_prompt(x, pos, w1, w2, nc_pad):
    b, t, w = x.shape
    spec = pl.BlockSpec((1, t, w), lambda bi, c: (bi, 0, 0))
    return compress([x], [spec], (b, 1), [], t, pos, w1, w2, nc_pad, interleaved=False)


def compress_paged(cache, page_table, pos, w1, w2, nc_pad, pages_per_step=32):
    b, n_pages = page_table.shape
    page = cache.shape[1] // NSA_KV
    pps = min(pages_per_step, n_pages)
    assert n_pages % pps == 0

    def spec(i):
        return pl.BlockSpec((1, page * NSA_KV, HEAD_DIM), lambda bi, c, pt: (pt[bi, c * pps + i], 0, 0))

    return compress([cache] * pps, [spec(i) for i in range(pps)], (b, n_pages // pps), [page_table],
                    n_pages * page, pos, w1, w2, nc_pad, interleaved=True)


def _select_blocks(imp, pos_q, n_blocks):
    t, w = imp.shape
    blk = lax.broadcasted_iota(jnp.int32, (t, w), 1)
    complete = blk * BLK + (BLK - 1) <= pos_q
    cur = blk == pos_q // BLK
    val = jnp.where(cur, BIG_SEL, jnp.where(blk == 0, 0.5 * BIG_SEL, jnp.where(complete, imp, NEG)))
    val = jnp.where(blk < n_blocks, val, -3e38)
    rank = jnp.zeros((t, w), jnp.int32)
    for i in range(n_blocks):
        vi = val[:, i:i + 1]
        ahead = (vi > val) | ((vi == val) & (blk > i))
        rank = rank + ahead.astype(jnp.int32)
    return (rank < min(N_SELECT, n_blocks)) & (blk < n_blocks)


def _expand_blocks(sel, first_block, n_keys):
    w = sel.shape[1]
    blk = lax.broadcasted_iota(jnp.int32, (w, n_keys), 0)
    key = lax.broadcasted_iota(jnp.int32, (w, n_keys), 1)
    onehot = jnp.where(blk == first_block + key // BLK, 1.0, 0.0).astype(BF16)
    return jnp.dot(sel, onehot, preferred_element_type=F32)


def _head_rows(q_ref, g, hpg):
    parts = [q_ref[0, :, (g * hpg + j) * HEAD_DIM:(g * hpg + j + 1) * HEAD_DIM] for j in range(hpg)]
    return (jnp.concatenate(parts, axis=0) * (HEAD_DIM ** -0.5 * LOG2E)).astype(BF16)


def _compressed_branch(q, kcb, vcb, slopes, pos_q, nc, hpg):
    t = pos_q.shape[0]
    w = kcb.shape[0]
    blk_end = lax.broadcasted_iota(jnp.int32, (t, w), 1) * BLK + (BLK - 1)
    dist = pos_q - blk_end
    mask = ((dist >= 0) & (blk_end < nc * BLK))[None]
    s = _nt_dot(q, kcb.astype(BF16)).reshape(hpg, t, w) - slopes * dist.astype(F32)[None]
    s = jnp.where(mask, s, NEG)
    e = jnp.where(mask, jnp.exp2(s - jnp.max(s, axis=-1, keepdims=True)), 0.0)
    p = e / jnp.maximum(jnp.sum(e, axis=-1, keepdims=True), 1e-30)
    o = jnp.dot(p.reshape(hpg * t, w).astype(BF16), vcb.astype(BF16), preferred_element_type=F32)
    return o, jnp.sum(p, axis=0)


def _merge_heads(o_ref, gates, g, hpg, o_cmp, o_slc, o_win):
    t = o_cmp.shape[0] // hpg
    for j in range(hpg):
        head = g * hpg + j
        rows = slice(j * t, (j + 1) * t)
        o = (gates[:, 3 * head:3 * head + 1] * o_cmp[rows]
             + gates[:, 3 * head + 1:3 * head + 2] * o_slc[rows]
             + gates[:, 3 * head + 2:3 * head + 3] * o_win[rows])
        o_ref[0, :, head * HEAD_DIM:(head + 1) * HEAD_DIM] = o.astype(o_ref.dtype)


def _nsa_prompt_body(slope_ref, q_ref, kcb_ref, vcb_ref, ks_ref, vs_ref, kw_ref, vw_ref, gate_ref, o_ref,
                     *, tq, ck, wlen, nc, hpg):
    qi = pl.program_id(1)
    q0 = qi * tq
    pos_q = q0 + lax.broadcasted_iota(jnp.int32, (tq, 1), 0)
    gates = gate_ref[0]
    rel_s = pos_q - lax.broadcasted_iota(jnp.int32, (tq, ck), 1)
    col_s = lax.broadcasted_iota(jnp.int32, (1, 1, ck), 2).astype(F32)
    w0 = pl.multiple_of(jnp.maximum(q0 + tq - wlen, 0), tq)
    dist_w = pos_q - (w0 + lax.broadcasted_iota(jnp.int32, (tq, wlen), 1))
    mask_w = ((dist_w >= 0) & (dist_w <= WINDOW))[None]
    pos_w = (w0 - q0 + lax.broadcasted_iota(jnp.int32, (1, 1, wlen), 2)).astype(F32)
    init = _softmax_init((hpg, tq), hpg * tq, HEAD_DIM)
    for g in range(NSA_KV):
        lanes = pl.ds(g * HEAD_DIM, HEAD_DIM)
        q = _head_rows(q_ref, g, hpg)
        slopes = _head_slopes(slope_ref, g * hpg, hpg)
        o_cmp, imp = _compressed_branch(q, kcb_ref[0, g], vcb_ref[0, g], slopes, pos_q, nc, hpg)
        sel = jnp.where(_select_blocks(imp, pos_q, nc), 1.0, 0.0).astype(BF16)

        def slc_chunk(c, carry, causal):
            k0 = pl.multiple_of(c * ck, ck)
            k = ks_ref[0, pl.ds(k0, ck), lanes].astype(BF16)
            v = vs_ref[0, pl.ds(k0, ck), lanes].astype(BF16)
            mask = _expand_blocks(sel, k0 // BLK, ck) > 0.5
            if causal:
                mask = mask & (rel_s >= k0)
            s = _nt_dot(q, k).reshape(hpg, tq, ck) + slopes * (col_s + (k0 - q0).astype(F32))
            return _softmax_update(carry, s, jnp.broadcast_to(mask[None], s.shape), v, zero_masked=False)

        n_full = q0 // ck
        carry = lax.fori_loop(0, n_full, functools.partial(slc_chunk, causal=False), init)
        carry = lax.fori_loop(n_full, (q0 + tq + ck - 1) // ck, functools.partial(slc_chunk, causal=True), carry)
        o_slc = _softmax_finish(carry)

        kw = kw_ref[0, pl.ds(w0, wlen), lanes].astype(BF16)
        vw = vw_ref[0, pl.ds(w0, wlen), lanes].astype(BF16)
        s = _nt_dot(q, kw).reshape(hpg, tq, wlen) + slopes * pos_w
        o_win = _softmax_finish(_softmax_update(init, s, jnp.broadcast_to(mask_w, s.shape), vw, zero_masked=False))
        _merge_heads(o_ref, gates, g, hpg, o_cmp, o_slc, o_win)


def nsa_attention_prompt(q, kcb, vcb, ks, vs, kw, vw, gates, tq=128, ck=512):
    b, t, hd = q.shape
    n_heads = hd // HEAD_DIM
    hpg = n_heads // NSA_KV
    tq, ck = min(tq, t), min(ck, t)
    wlen = min(WINDOW + tq, t)
    assert ck % tq == 0 and WINDOW % tq == 0 and N_SELECT >= 2
    tok = pl.BlockSpec((1, tq, hd), lambda bi, qi: (bi, qi, 0))
    cmp_spec = pl.BlockSpec((1,) + kcb.shape[1:], lambda bi, qi: (bi, 0, 0, 0))
    seq = pl.BlockSpec((1, t, NSA_KV * HEAD_DIM), lambda bi, qi: (bi, 0, 0))
    return pl.pallas_call(
        functools.partial(_nsa_prompt_body, tq=tq, ck=ck, wlen=wlen, nc=t // BLK, hpg=hpg),
        grid=(b, t // tq),
        in_specs=[_smem_spec(), tok, cmp_spec, cmp_spec, seq, seq, seq, seq,
                  pl.BlockSpec((1, tq, LANES), lambda bi, qi: (bi, qi, 0))],
        out_specs=tok,
        out_shape=jax.ShapeDtypeStruct((b, t, hd), BF16),
        compiler_params=_params("arbitrary", "arbitrary"),
        name="nsa_attention_prompt",
    )(_alibi_slopes(n_heads), q, kcb, vcb, ks, vs, kw, vw, gates)


def _nsa_sample_body(pt_ref, slope_ref, q_ref, kcb_ref, vcb_ref, ksn_ref, vsn_ref, kwn_ref, vwn_ref,
                     wk_ref, wv_ref, gate_ref, *rest, pps, page, t_new, past, hpg, sel_w):
    k_pages = rest[:pps]
    v_pages = rest[pps:2 * pps]
    o_ref = rest[2 * pps]
    q_s, ocmp_s, sel_s, m_s, l_s, acc_s = rest[2 * pps + 1:]
    c, n_c = pl.program_id(1), pl.num_programs(1)
    rows = hpg * t_new
    nc = past // BLK
    n_blocks = -(-(past + t_new) // BLK)
    pos_q = past + lax.broadcasted_iota(jnp.int32, (t_new, 1), 0)
    keys_step = pps * page

    @pl.when(c == 0)
    def _():
        for g in range(NSA_KV):
            q = _head_rows(q_ref, g, hpg)
            q_s[g] = q
            slopes = _head_slopes(slope_ref, g * hpg, hpg)
            o_cmp, imp = _compressed_branch(q, kcb_ref[0, g], vcb_ref[0, g], slopes, pos_q, nc, hpg)
            ocmp_s[g] = o_cmp
            imp = jnp.concatenate([imp, jnp.zeros((t_new, sel_w - imp.shape[1]), F32)], axis=1)
            sel_s[g] = jnp.where(_select_blocks(imp, pos_q, n_blocks), 1.0, 0.0)
        m_s[...] = jnp.full(m_s.shape, NEG, F32)
        l_s[...] = jnp.zeros(l_s.shape, F32)
        acc_s[...] = jnp.zeros(acc_s.shape, F32)

    def group_rows(ref, g):
        return ref[0, pl.ds(g, page, stride=NSA_KV), :]

    rel = pos_q - lax.broadcasted_iota(jnp.int32, (t_new, keys_step), 1)
    for g in range(NSA_KV):
        slopes = _head_slopes(slope_ref, g * hpg, hpg)
        q = q_s[g]
        k0 = c * keys_step
        dist = rel - k0
        mask = (_expand_blocks(sel_s[g].astype(BF16), k0 // BLK, keys_step) > 0.5)[None]
        k = jnp.concatenate([group_rows(k_pages[i], g) for i in range(pps)], axis=0).astype(BF16)
        v = jnp.concatenate([group_rows(v_pages[i], g) for i in range(pps)], axis=0).astype(BF16)
        s = _nt_dot(q, k).reshape(hpg, t_new, keys_step) - slopes * dist.astype(F32)[None]
        carry = _softmax_update((m_s[g], l_s[g], acc_s[g]), s, jnp.broadcast_to(mask, s.shape), v)
        m_s[g], l_s[g], acc_s[g] = carry

    @pl.when(c == n_c - 1)
    def _():
        gates = gate_ref[0]
        pad = jnp.zeros((page - t_new, HEAD_DIM), F32)
        col = lax.broadcasted_iota(jnp.int32, (t_new, page), 1)
        dist_new = (pos_q - past) - col
        for g in range(NSA_KV):
            lanes = pl.ds(g * HEAD_DIM, HEAD_DIM)
            slopes = _head_slopes(slope_ref, g * hpg, hpg)
            q = q_s[g]
            k = jnp.concatenate([ksn_ref[0, :, lanes], pad], axis=0).astype(BF16)
            v = jnp.concatenate([vsn_ref[0, :, lanes], pad], axis=0).astype(BF16)
            sel_new = _expand_blocks(sel_s[g].astype(BF16), nc, page) > 0.5
            mask = (sel_new & (dist_new >= 0) & (col < t_new))[None]
            s = _nt_dot(q, k).reshape(hpg, t_new, page) - slopes * dist_new.astype(F32)[None]
            carry = _softmax_update((m_s[g], l_s[g], acc_s[g]), s, jnp.broadcast_to(mask, s.shape), v)
            o_slc = _softmax_finish(carry)
            w_buf = wk_ref.shape[1]
            colw = lax.broadcasted_iota(jnp.int32, (t_new, w_buf), 1)
            dist_w = pos_q - (past - w_buf + colw)
            carry = _softmax_init((hpg, t_new), rows, HEAD_DIM)
            sw = _nt_dot(q, wk_ref[0, :, lanes].astype(BF16)).reshape(hpg, t_new, w_buf)
            sw = sw - slopes * dist_w.astype(F32)[None]
            mask_w = ((dist_w >= 0) & (dist_w <= WINDOW))[None]
            carry = _softmax_update(carry, sw, jnp.broadcast_to(mask_w, sw.shape),
                                    wv_ref[0, :, lanes].astype(BF16))
            k = jnp.concatenate([kwn_ref[0, :, lanes], pad], axis=0).astype(BF16)
            v = jnp.concatenate([vwn_ref[0, :, lanes], pad], axis=0).astype(BF16)
            mask_n = ((dist_new >= 0) & (dist_new <= WINDOW) & (col < t_new))[None]
            sn = _nt_dot(q, k).reshape(hpg, t_new, page) - slopes * dist_new.astype(F32)[None]
            carry = _softmax_update(carry, sn, jnp.broadcast_to(mask_n, sn.shape), v)
            o_win = _softmax_finish(carry)
            _merge_heads(o_ref, gates, g, hpg, ocmp_s[g], o_slc, o_win)


def nsa_attention_sample(q, kcb, vcb, ks_new, vs_new, kw_new, vw_new, win_k, win_v, gates,
                         cache_k, cache_v, page_table, pages_per_step=8):
    b, t, hd = q.shape
    n_heads = hd // HEAD_DIM
    hpg = n_heads // NSA_KV
    n_pages = page_table.shape[1]
    page = cache_k.shape[1] // NSA_KV
    past = n_pages * page
    pps = min(pages_per_step, n_pages)
    n_blocks = -(-(past + t) // BLK)
    sel_w = -(-n_blocks // LANES) * LANES
    assert n_pages % pps == 0 and t <= page and page % BLK == 0 and kcb.shape[2] * BLK >= past
    width = NSA_KV * HEAD_DIM
    tok = pl.BlockSpec((1, t, hd), lambda bi, c, pt: (bi, 0, 0))
    new = pl.BlockSpec((1, t, width), lambda bi, c, pt: (bi, 0, 0))
    cmp_spec = pl.BlockSpec((1,) + kcb.shape[1:], lambda bi, c, pt: (bi, 0, 0, 0))
    win = pl.BlockSpec((1,) + win_k.shape[1:], lambda bi, c, pt: (bi, 0, 0))

    def page_spec(i):
        return pl.BlockSpec((1, page * NSA_KV, HEAD_DIM), lambda bi, c, pt: (pt[bi, c * pps + i], 0, 0))

    pages = [page_spec(i) for i in range(pps)]
    rows = hpg * t
    grid_spec = pltpu.PrefetchScalarGridSpec(
        num_scalar_prefetch=1,
        grid=(b, n_pages // pps),
        in_specs=[_smem_spec(), tok, cmp_spec, cmp_spec, new, new, new, new, win, win,
                  pl.BlockSpec((1, t, LANES), lambda bi, c, pt: (bi, 0, 0))] + pages + pages,
        out_specs=tok,
        scratch_shapes=[pltpu.VMEM((NSA_KV, rows, HEAD_DIM), BF16),
                        pltpu.VMEM((NSA_KV, rows, HEAD_DIM), F32),
                        pltpu.VMEM((NSA_KV, t, sel_w), F32),
                        pltpu.VMEM((NSA_KV, hpg, t, 1), F32),
                        pltpu.VMEM((NSA_KV, hpg, t, 1), F32),
                        pltpu.VMEM((NSA_KV, rows, HEAD_DIM), F32)])
    return pl.pallas_call(
        functools.partial(_nsa_sample_body, pps=pps, page=page, t_new=t, past=past, hpg=hpg, sel_w=sel_w),
        grid_spec=grid_spec,
        out_shape=jax.ShapeDtypeStruct((b, t, hd), BF16),
        compiler_params=_params("arbitrary", "arbitrary"),
        name="nsa_attention_sample",
    )(page_table, _alibi_slopes(n_heads), q, kcb, vcb, ks_new, vs_new, kw_new, vw_new, win_k, win_v, gates,
      *([cache_k] * pps), *([cache_v] * pps))


def _last_rows(a, n):
    a = jnp.pad(a, ((0, 0), (n, 0)) + ((0, 0),) * (a.ndim - 2))
    return a[:, a.shape[1] - n:]


def _project(x, mods, norm_g, w_in_t, mix_diff, mix_nsa):
    b, t, d = x.shape
    h = modulate(x, mods[0], mods[1], norm_g)
    seg = lambda col0, n: matmul(h, w_in_t, col0=col0, ncols=n, wt=True).reshape(b, t, n)
    dq, dk, dv = seg(0, mix_diff), seg(mix_diff, mix_diff), seg(2 * mix_diff, mix_diff)
    nq = seg(3 * mix_diff, mix_nsa)
    n_gate = 3 * (mix_nsa // HEAD_DIM)
    outs = kv_gate_projection(h, w_in_t, 3 * mix_diff + mix_nsa, 6, n_gate)
    kvs = [o.reshape(b, t, NSA_KV * HEAD_DIM) for o in outs[:6]]
    gates = outs[6].reshape(b, t, LANES)
    return dq, dk, dv, nq, kvs, gates


def _finish(x, o_diff, o_nsa, mods, norm2_g, w_out, w_up, w_down, final_g):
    b, t, d = x.shape
    mix_in = jnp.concatenate([o_diff, o_nsa], axis=-1).reshape(b * t, -1)
    mix = matmul(mix_in, w_out)
    x1, h2 = residual_modulate(x, mix, mods[2], mods[3], mods[4], norm2_g)
    u = matmul(h2, w_up, out_dtype=BF16, relu2=True)
    ffn = matmul(u, w_down, tn=1024, tk=2048)
    return residual_final_norm(x1, ffn, mods[5], final_g)


def kernel(x_prompt, x_sample, c_prompt, c_sample, cache_diff_k, cache_diff_v, cache_cmp_k, cache_cmp_v, cache_slc_k, cache_slc_v, state_win_k, state_win_v, page_table, norm1_g, norm2_g, ada_w, ada_b, w_in, w_out, diff_lq1, diff_lk1, diff_lq2, diff_lk2, diff_subln_g, cmp_k_pos, cmp_k_w1, cmp_k_w2, cmp_v_pos, cmp_v_w1, cmp_v_w2, w_up, w_down, final_g):
    depth = ada_w.shape[0]
    assert depth == 1
    l = 0
    bp, tp, d = x_prompt.shape
    bs, ts, _ = x_sample.shape
    n_phys, page, n_heads_diff, _ = cache_diff_k.shape[1:]
    mix_diff = n_heads_diff * HEAD_DIM
    mix_nsa = d - mix_diff
    w_buf = state_win_k.shape[2]
    width = NSA_KV * HEAD_DIM
    lam_init = 0.8 - 0.6 * math.exp(-0.3 * l)
    lams = [a[l].reshape(1, DIFF_HALF) for a in (diff_lq1, diff_lk1, diff_lq2, diff_lk2)]
    subln = diff_subln_g[l].reshape(1, HEAD_DIM)

    n_c = bp + bs
    n_c_pad = -(-n_c // 16) * 16
    c_all = jnp.concatenate([c_prompt, c_sample, jnp.zeros((n_c_pad - n_c, d), F32)], axis=0)
    mod = ada_modulation(c_all, ada_w[l], ada_b[l]).reshape(n_c_pad, 6, 1, d)
    mods_p = [mod[:bp, i] for i in range(6)]
    mods_s = [mod[bp:n_c, i] for i in range(6)]

    w_in_t = jnp.swapaxes(w_in[l], 0, 1)

    dq, dk_p, dv_p, nq, kvs_p, gates = _project(x_prompt, mods_p, norm1_g[l], w_in_t, mix_diff, mix_nsa)
    o_diff = diff_attention_prompt(dq, dk_p, dv_p, lams, subln, lam_init)
    nc_pad = -(-max(tp, page_table.shape[1] * page) // (BLK * LANES)) * LANES
    kcb = compress_prompt(kvs_p[0], cmp_k_pos[l], cmp_k_w1[l], cmp_k_w2[l], nc_pad)
    vcb = compress_prompt(kvs_p[1], cmp_v_pos[l], cmp_v_w1[l], cmp_v_w2[l], nc_pad)
    o_nsa = nsa_attention_prompt(nq, kcb, vcb, kvs_p[2], kvs_p[3], kvs_p[4], kvs_p[5], gates)
    y_prompt = _finish(x_prompt, o_diff, o_nsa, mods_p, norm2_g[l], w_out[l], w_up[l], w_down[l], final_g)

    dq, dk_s, dv_s, nq, kvs_s, gates = _project(x_sample, mods_s, norm1_g[l], w_in_t, mix_diff, mix_nsa)
    o_diff = diff_attention_sample(dq, dk_s, dv_s, cache_diff_k[l], cache_diff_v[l], page_table, lams, subln, lam_init)
    flat_nsa = lambda c: c[l].reshape(n_phys, page * NSA_KV, HEAD_DIM)
    kcb = compress_paged(flat_nsa(cache_cmp_k), page_table, cmp_k_pos[l], cmp_k_w1[l], cmp_k_w2[l], nc_pad)
    vcb = compress_paged(flat_nsa(cache_cmp_v), page_table, cmp_v_pos[l], cmp_v_w1[l], cmp_v_w2[l], nc_pad)
    win_k = state_win_k[l].reshape(bs, w_buf, width)
    win_v = state_win_v[l].reshape(bs, w_buf, width)
    o_nsa = nsa_attention_sample(nq, kcb, vcb, kvs_s[2], kvs_s[3], kvs_s[4], kvs_s[5], win_k, win_v, gates,
                                 flat_nsa(cache_slc_k), flat_nsa(cache_slc_v), page_table)
    y_sample = _finish(x_sample, o_diff, o_nsa, mods_s, norm2_g[l], w_out[l], w_up[l], w_down[l], final_g)

    heads = lambda a, b, t, h: a.reshape(1, b, t, h, HEAD_DIM)
    out_p = [heads(dk_p, bp, tp, n_heads_diff), heads(dv_p, bp, tp, n_heads_diff)]
    out_p += [heads(a, bp, tp, NSA_KV) for a in kvs_p[:4]]
    out_p += [heads(_last_rows(a, w_buf), bp, w_buf, NSA_KV) for a in kvs_p[4:]]
    out_s = [heads(dk_s, bs, ts, n_heads_diff), heads(dv_s, bs, ts, n_heads_diff)]
    out_s += [heads(a, bs, ts, NSA_KV) for a in kvs_s[:4]]
    out_s += [heads(jnp.concatenate([w, a], axis=1)[:, ts:], bs, w_buf, NSA_KV)
              for w, a in ((win_k, kvs_s[4]), (win_v, kvs_s[5]))]
    return (y_prompt, y_sample, *out_p, *out_s)
```

```python
import functools
import math

import numpy as np
import jax
import jax.numpy as jnp
from jax import lax
from jax.experimental import pallas as pl
from jax.experimental.pallas import tpu as pltpu

F32 = jnp.float32
BF16 = jnp.bfloat16

DIFF_HALF = 64
HEAD_DIM = 128
NSA_KV = 2
BLK = 64
N_SELECT = 16
WINDOW = 512
EPS = 1e-6
NEG = -1e30
BIG_SEL = 1e4
LOG2E = math.log2(math.e)
LANES = 128
SUBLANES = 8
BLK_PITCH = BLK + SUBLANES
VMEM_LIMIT_BYTES = 56 * 1024 * 1024


def _params(*sem):
    return pltpu.CompilerParams(dimension_semantics=sem, vmem_limit_bytes=VMEM_LIMIT_BYTES)


def _alibi_slopes(n):
    return jnp.asarray(np.exp2(-8.0 * np.arange(1, n + 1) / n), dtype=F32)


def _silu(x):
    return x * jax.nn.sigmoid(x)


def _smem_spec():
    return pl.BlockSpec(memory_space=pltpu.SMEM)


def _ada_body(c_ref, w_ref, b_ref, o_ref):
    a = _silu(c_ref[...]).astype(BF16)
    o_ref[...] = jnp.dot(a, w_ref[...].astype(BF16), preferred_element_type=F32) + b_ref[...]


def ada_modulation(c, w, b, tn=512):
    m, k = c.shape
    n = w.shape[1]
    tn = min(tn, n)
    return pl.pallas_call(
        _ada_body,
        grid=(n // tn,),
        in_specs=[pl.BlockSpec((m, k), lambda j: (0, 0)),
                  pl.BlockSpec((k, tn), lambda j: (0, j)),
                  pl.BlockSpec((1, tn), lambda j: (0, j))],
        out_specs=pl.BlockSpec((m, tn), lambda j: (0, j)),
        out_shape=jax.ShapeDtypeStruct((m, n), F32),
        compiler_params=_params("arbitrary"),
        name="ada_modulation",
    )(c, w, b.reshape(1, n))


def _rms(x, g):
    return x * lax.rsqrt(jnp.mean(x * x, axis=-1, keepdims=True) + EPS) * g


def _modulate_body(x_ref, sh_ref, sc_ref, g_ref, h_ref):
    h = _rms(x_ref[...], g_ref[...]) * (1.0 + sc_ref[...]) + sh_ref[...]
    h_ref[...] = h.reshape(h_ref.shape).astype(h_ref.dtype)


def _mid_body(x_ref, mix_ref, g1_ref, sh_ref, sc_ref, g_ref, x1_ref, h_ref):
    x1 = x_ref[...] + g1_ref[...] * mix_ref[...].reshape(x_ref.shape)
    x1_ref[...] = x1
    h = _rms(x1, g_ref[...]) * (1.0 + sc_ref[...]) + sh_ref[...]
    h_ref[...] = h.reshape(h_ref.shape).astype(h_ref.dtype)


def _final_body(x_ref, ffn_ref, g2_ref, g_ref, y_ref):
    x2 = x_ref[...] + g2_ref[...] * ffn_ref[...].reshape(x_ref.shape)
    y_ref[...] = _rms(x2, g_ref[...])


def _row_tiles(b, t):
    if t >= 256:
        return 1, 256
    return b, t


def _row_specs(b, t, d):
    bb, tt = _row_tiles(b, t)
    nt = t // tt
    grid = (b // bb, nt)
    x3 = pl.BlockSpec((bb, tt, d), lambda i, j: (i, j, 0))
    mod = pl.BlockSpec((bb, 1, d), lambda i, j: (i, 0, 0))
    gain = pl.BlockSpec((1, 1, d), lambda i, j: (0, 0, 0))
    flat = pl.BlockSpec((bb * tt, d), lambda i, j: (i * nt + j, 0))
    return grid, x3, mod, gain, flat


def modulate(x, shift, scale, g):
    b, t, d = x.shape
    grid, x3, mod, gain, flat = _row_specs(b, t, d)
    return pl.pallas_call(
        _modulate_body, grid=grid,
        in_specs=[x3, mod, mod, gain], out_specs=flat,
        out_shape=jax.ShapeDtypeStruct((b * t, d), BF16),
        compiler_params=_params("arbitrary", "arbitrary"),
        name="modulate",
    )(x, shift, scale, g.reshape(1, 1, d))


def residual_modulate(x, mix, gate1, shift, scale, g):
    b, t, d = x.shape
    grid, x3, mod, gain, flat = _row_specs(b, t, d)
    return pl.pallas_call(
        _mid_body, grid=grid,
        in_specs=[x3, flat, mod, mod, mod, gain], out_specs=[x3, flat],
        out_shape=[jax.ShapeDtypeStruct((b, t, d), F32), jax.ShapeDtypeStruct((b * t, d), BF16)],
        compiler_params=_params("arbitrary", "arbitrary"),
        name="residual_modulate",
    )(x, mix, gate1, shift, scale, g.reshape(1, 1, d))


def residual_final_norm(x, ffn, gate2, g):
    b, t, d = x.shape
    grid, x3, mod, gain, flat = _row_specs(b, t, d)
    return pl.pallas_call(
        _final_body, grid=grid,
        in_specs=[x3, flat, mod, gain], out_specs=x3,
        out_shape=jax.ShapeDtypeStruct((b, t, d), F32),
        compiler_params=_params("arbitrary", "arbitrary"),
        name="residual_final_norm",
    )(x, ffn, gate2, g.reshape(1, 1, d))


def _nt_dot(a, b):
    return lax.dot_general(a, b, (((1,), (1,)), ((), ())), preferred_element_type=F32)


def _w_dot(a, w, w_rows_are_outputs):
    w = w.astype(BF16)
    return _nt_dot(a, w) if w_rows_are_outputs else jnp.dot(a, w, preferred_element_type=F32)


def _mm_body(a_ref, w_ref, o_ref, *scratch, nk, relu2, wt):
    part = _w_dot(a_ref[...], w_ref[...], wt)

    def finish(r):
        if relu2:
            r = jnp.square(jnp.maximum(r, 0.0))
        o_ref[...] = r.astype(o_ref.dtype)

    if nk == 1:
        finish(part)
    else:
        acc_ref, = scratch
        k = pl.program_id(2)

        @pl.when(k == 0)
        def _():
            acc_ref[...] = part

        @pl.when(k > 0)
        def _():
            acc_ref[...] += part

        @pl.when(k == nk - 1)
        def _():
            finish(acc_ref[...])


def matmul(a, w, *, col0=0, ncols=None, tm=1024, tn=512, tk=None, out_dtype=F32, relu2=False, wt=False):
    m, k = a.shape
    n_total = w.shape[0] if wt else w.shape[1]
    ncols = n_total - col0 if ncols is None else ncols
    tm, tn = min(tm, m), min(tn, ncols)
    tk = k if tk is None else min(tk, k)
    assert m % tm == 0 and ncols % tn == 0 and k % tk == 0 and col0 % tn == 0
    nk, jb = k // tk, col0 // tn
    if wt:
        w_spec = pl.BlockSpec((tn, tk), lambda i, j, kk: (jb + j, kk))
    else:
        w_spec = pl.BlockSpec((tk, tn), lambda i, j, kk: (kk, jb + j))
    return pl.pallas_call(
        functools.partial(_mm_body, nk=nk, relu2=relu2, wt=wt),
        grid=(m // tm, ncols // tn, nk),
        in_specs=[pl.BlockSpec((tm, tk), lambda i, j, kk: (i, kk)), w_spec],
        out_specs=pl.BlockSpec((tm, tn), lambda i, j, kk: (i, j)),
        out_shape=jax.ShapeDtypeStruct((m, ncols), out_dtype),
        scratch_shapes=[pltpu.VMEM((tm, tn), F32)] if nk > 1 else [],
        compiler_params=_params("arbitrary", "arbitrary", "arbitrary"),
        name="matmul",
    )(a, w)


def _mm_split_body(a0_ref, a1_ref, w_ref, o_ref):
    k0 = a0_ref.shape[1]
    o_ref[...] = (_w_dot(a0_ref[...], w_ref[:k0, :], False) + _w_dot(a1_ref[...], w_ref[k0:, :], False))


def matmul_split(a0, a1, w, tm=1024, tn=512):
    m, k0 = a0.shape
    k1, n = a1.shape[1], w.shape[1]
    tm, tn = min(tm, m), min(tn, n)
    assert m % tm == 0 and n % tn == 0 and w.shape[0] == k0 + k1
    return pl.pallas_call(
        _mm_split_body,
        grid=(m // tm, n // tn),
        in_specs=[pl.BlockSpec((tm, k0), lambda i, j: (i, 0)),
                  pl.BlockSpec((tm, k1), lambda i, j: (i, 0)),
                  pl.BlockSpec((k0 + k1, tn), lambda i, j: (0, j))],
        out_specs=pl.BlockSpec((tm, tn), lambda i, j: (i, j)),
        out_shape=jax.ShapeDtypeStruct((m, n), F32),
        compiler_params=_params("arbitrary", "arbitrary"),
        name="matmul_split",
    )(a0, a1, w)


def _kv_gate_body(a_ref, w_ref, *o_refs, n_kv, n_gate):
    j = pl.program_id(1)
    r = _w_dot(a_ref[...], w_ref[...], True)
    for n in range(n_kv):
        @pl.when(j == n)
        def _(n=n):
            o_refs[n][...] = r

    @pl.when(j == n_kv)
    def _():
        g = r[:, :LANES]
        col = lax.broadcasted_iota(jnp.int32, g.shape, 1)
        o_refs[n_kv][...] = jnp.where(col < n_gate, jax.nn.sigmoid(g), 0.0)


def kv_gate_projection(a, w, col0, n_kv, n_gate, tm=1024):
    m, k = a.shape
    tm = min(tm, m)
    width = 2 * LANES
    assert col0 % width == 0 and n_gate <= LANES
    jb = col0 // width
    kv = pl.BlockSpec((tm, width), lambda i, j: (i, 0))
    return pl.pallas_call(
        functools.partial(_kv_gate_body, n_kv=n_kv, n_gate=n_gate),
        grid=(m // tm, n_kv + 1),
        in_specs=[pl.BlockSpec((tm, k), lambda i, j: (i, 0)),
                  pl.BlockSpec((width, k), lambda i, j: (jb + j, 0))],
        out_specs=[kv] * n_kv + [pl.BlockSpec((tm, LANES), lambda i, j: (i, 0))],
        out_shape=[jax.ShapeDtypeStruct((m, 2 * LANES), F32)] * n_kv + [jax.ShapeDtypeStruct((m, LANES), F32)],
        compiler_params=_params("arbitrary", "arbitrary"),
        name="kv_gate_projection",
    )(a, w)


def _softmax_update(carry, s, mask, v, zero_masked=True):
    m, l, acc = carry
    if mask is not None:
        s = jnp.where(mask, s, NEG)
    m_new = jnp.maximum(m, jnp.max(s, axis=-1, keepdims=True))
    p = jnp.exp2(s - m_new)
    if mask is not None and zero_masked:
        p = jnp.where(mask, p, 0.0)
    alpha = jnp.exp2(m - m_new)
    l = alpha * l + jnp.sum(p, axis=-1, keepdims=True)
    rows = acc.shape[0]
    pv = jnp.dot(p.reshape(rows, p.shape[-1]).astype(BF16), v, preferred_element_type=F32)
    acc = alpha.reshape(rows, 1) * acc + pv
    return m_new, l, acc


def _softmax_init(lead, rows_total, d):
    return (jnp.full(lead + (1,), NEG, F32), jnp.zeros(lead + (1,), F32), jnp.zeros((rows_total, d), F32))


def _softmax_finish(carry):
    _, l, acc = carry
    return acc / jnp.maximum(l.reshape(acc.shape[0], 1), 1e-30)


def _lambda_full(lq1, lk1, lq2, lk2, lam_init):
    a = jnp.sum(lq1 * lk1, axis=-1, keepdims=True)
    b = jnp.sum(lq2 * lk2, axis=-1, keepdims=True)
    return jnp.exp(a) - jnp.exp(b) + lam_init


def _diff_qcat(q):
    lane = lax.broadcasted_iota(jnp.int32, q.shape, 1)
    qs = q * (DIFF_HALF ** -0.5 * LOG2E)
    first = jnp.where(lane < DIFF_HALF, qs, 0.0)
    second = jnp.where(lane >= DIFF_HALF, qs, 0.0)
    return jnp.concatenate([first, second], axis=0).astype(BF16)


def _diff_finish(o12, lam, lam_init, subln):
    t = o12.shape[0] // 2
    o = o12[:t] - lam * o12[t:]
    return _rms(o, subln) * (1.0 - lam_init)


def _diff_prompt_body(slope_ref, q_ref, k_ref, v_ref, lq1, lk1, lq2, lk2, subln_ref, o_ref, *, tq, ck, lam_init):
    h, qi = pl.program_id(1), pl.program_id(2)
    slope = slope_ref[h] * LOG2E
    qcat = _diff_qcat(q_ref[0])
    q0 = qi * tq
    row = lax.broadcasted_iota(jnp.int32, (2 * tq, 1), 0)
    pos_q = q0 + jnp.where(row >= tq, row - tq, row)
    col = lax.broadcasted_iota(jnp.int32, (1, ck), 1)
    colf = col.astype(F32)

    def chunk(c, carry, causal):
        k0 = pl.multiple_of(c * ck, ck)
        k = k_ref[0, pl.ds(k0, ck), :].astype(BF16)
        v = v_ref[0, pl.ds(k0, ck), :].astype(BF16)
        s = _nt_dot(qcat, k) + slope * (colf + (k0 - q0).astype(F32))
        return _softmax_update(carry, s, (pos_q >= k0 + col) if causal else None, v, zero_masked=False)

    n_full = q0 // ck
    n_chunks = (q0 + tq + ck - 1) // ck
    carry = _softmax_init((2 * tq,), 2 * tq, HEAD_DIM)
    carry = lax.fori_loop(0, n_full, functools.partial(chunk, causal=False), carry)
    carry = lax.fori_loop(n_full, n_chunks, functools.partial(chunk, causal=True), carry)
    lam = _lambda_full(lq1[...], lk1[...], lq2[...], lk2[...], lam_init)
    o_ref[0] = _diff_finish(_softmax_finish(carry), lam, lam_init, subln_ref[...]).astype(o_ref.dtype)


def diff_attention_prompt(q, k, v, lams, subln, lam_init, tq=256, ck=512):
    b, t, hd = q.shape
    n_heads = hd // HEAD_DIM
    tq, ck = min(tq, t), min(ck, t)
    assert ck % tq == 0
    qspec = pl.BlockSpec((1, tq, HEAD_DIM), lambda bi, h, qi: (bi, qi, h))
    kvspec = pl.BlockSpec((1, t, HEAD_DIM), lambda bi, h, qi: (bi, 0, h))
    vec = pl.BlockSpec((1, DIFF_HALF), lambda bi, h, qi: (0, 0))
    return pl.pallas_call(
        functools.partial(_diff_prompt_body, tq=tq, ck=ck, lam_init=lam_init),
        grid=(b, n_heads, t // tq),
        in_specs=[_smem_spec(), qspec, kvspec, kvspec, vec, vec, vec, vec,
                  pl.BlockSpec((1, HEAD_DIM), lambda bi, h, qi: (0, 0))],
        out_specs=qspec,
        out_shape=jax.ShapeDtypeStruct((b, t, hd), BF16),
        compiler_params=_params("arbitrary", "arbitrary", "arbitrary"),
        name="diff_attention_prompt",
    )(_alibi_slopes(n_heads), q, k, v, *lams, subln)


def _head_slopes(slope_ref, first, n):
    j = lax.broadcasted_iota(jnp.int32, (n, 1, 1), 0)
    out = jnp.zeros((n, 1, 1), F32)
    for jj in range(n):
        out = jnp.where(j == jj, slope_ref[first + jj] * LOG2E, out)
    return out


def _diff_sample_body(pt_ref, slope_ref, q_ref, kn_ref, vn_ref, lq1, lk1, lq2, lk2, subln_ref, *rest,
                      pps, sub, n_heads, hg, page, t_new, past, lam_init):
    k_pages = rest[:pps]
    v_pages = rest[pps:2 * pps]
    o_ref = rest[2 * pps]
    qcat_ref, bias_ref, srow_ref, m_ref, l_ref, acc_ref = rest[2 * pps + 1:]
    c, n_c = pl.program_id(1), pl.num_programs(1)
    rows = 2 * t_new
    n_groups = n_heads // hg
    r_all, cols = hg * rows, page * hg

    @pl.when(c == 0)
    def _():
        row = lax.broadcasted_iota(jnp.int32, (r_all, cols), 0)
        col = lax.broadcasted_iota(jnp.int32, (r_all, cols), 1)
        row1 = lax.broadcasted_iota(jnp.int32, (r_all, 1), 0)
        same_head = (col % hg) == (row // rows)
        key = (col // hg).astype(F32)
        for gi in range(n_groups):
            heads = range(gi * hg, (gi + 1) * hg)
            qcat_ref[gi] = jnp.concatenate(
                [_diff_qcat(q_ref[0, :, h * HEAD_DIM:(h + 1) * HEAD_DIM]) for h in heads], axis=0)
            srow = jnp.zeros((r_all, 1), F32)
            for j, h in enumerate(heads):
                srow = jnp.where(row1 // rows == j, slope_ref[h] * LOG2E, srow)
            srow_ref[gi] = srow
            bias_ref[gi] = jnp.where(same_head, srow * key, NEG)
        m_ref[...] = jnp.full(m_ref.shape, NEG, F32)
        l_ref[...] = jnp.zeros(l_ref.shape, F32)
        acc_ref[...] = jnp.zeros(acc_ref.shape, F32)

    def page_rows(ref, gi):
        return ref[0, :, gi].reshape(cols, HEAD_DIM).astype(BF16)

    for i0 in range(0, pps, sub):
        pages_now = range(i0, i0 + sub)
        scores = []
        for gi in range(n_groups):
            parts = []
            for i in pages_now:
                base = ((c * pps + i) * page - past).astype(F32)
                parts.append(_nt_dot(qcat_ref[gi], page_rows(k_pages[i], gi)) + bias_ref[gi] + srow_ref[gi] * base)
            scores.append(jnp.concatenate(parts, axis=1))
        for gi in range(n_groups):
            v = jnp.concatenate([page_rows(v_pages[i], gi) for i in pages_now], axis=0)
            m_ref[gi], l_ref[gi], acc_ref[gi] = _softmax_update(
                (m_ref[gi], l_ref[gi], acc_ref[gi]), scores[gi], None, v)

    @pl.when(c == n_c - 1)
    def _():
        lam = _lambda_full(lq1[...], lk1[...], lq2[...], lk2[...], lam_init)
        pad = jnp.zeros((page - t_new, HEAD_DIM), F32)
        r = lax.broadcasted_iota(jnp.int32, (rows, page), 0)
        cc = lax.broadcasted_iota(jnp.int32, (rows, page), 1)
        visible = (jnp.where(r >= t_new, r - t_new, r) >= cc) & (cc < t_new)
        for h in range(n_heads):
            gi, j = divmod(h, hg)
            rs = slice(j * rows, (j + 1) * rows)
            lanes = slice(h * HEAD_DIM, (h + 1) * HEAD_DIM)
            k = jnp.concatenate([kn_ref[0, :, lanes], pad], axis=0).astype(BF16)
            v = jnp.concatenate([vn_ref[0, :, lanes], pad], axis=0).astype(BF16)
            sn = _nt_dot(qcat_ref[gi, rs, :], k) + (slope_ref[h] * LOG2E) * cc.astype(F32)
            carry = _softmax_update((m_ref[gi, rs, :], l_ref[gi, rs, :], acc_ref[gi, rs, :]), sn, visible, v)
            o = _diff_finish(_softmax_finish(carry), lam, lam_init, subln_ref[...])
            o_ref[0, :, lanes] = o.astype(o_ref.dtype)


def diff_attention_sample(q, k_new, v_new, cache_k, cache_v, page_table, lams, subln, lam_init,
                          pages_per_step=8, pages_per_softmax_step=4):
    b, t, hd = q.shape
    n_heads = hd // HEAD_DIM
    n_phys, page = cache_k.shape[:2]
    n_pages = page_table.shape[1]
    hg = math.gcd(n_heads, SUBLANES)
    n_groups = n_heads // hg
    pps = min(pages_per_step, n_pages)
    sub = min(pages_per_softmax_step, pps)
    assert n_pages % pps == 0 and pps % sub == 0 and t <= page
    tok = pl.BlockSpec((1, t, hd), lambda bi, c, pt: (bi, 0, 0))
    vec = pl.BlockSpec((1, DIFF_HALF), lambda bi, c, pt: (0, 0))

    def page_spec(i):
        return pl.BlockSpec((1, page, n_groups, hg, HEAD_DIM), lambda bi, c, pt: (pt[bi, c * pps + i], 0, 0, 0, 0))

    pages = [page_spec(i) for i in range(pps)]
    r_all = hg * 2 * t
    grid_spec = pltpu.PrefetchScalarGridSpec(
        num_scalar_prefetch=1,
        grid=(b, n_pages // pps),
        in_specs=[_smem_spec(), tok, tok, tok, vec, vec, vec, vec,
                  pl.BlockSpec((1, HEAD_DIM), lambda bi, c, pt: (0, 0))] + pages + pages,
        out_specs=tok,
        scratch_shapes=[pltpu.VMEM((n_groups, r_all, HEAD_DIM), BF16),
                        pltpu.VMEM((n_groups, r_all, page * hg), F32),
                        pltpu.VMEM((n_groups, r_all, 1), F32),
                        pltpu.VMEM((n_groups, r_all, 1), F32),
                        pltpu.VMEM((n_groups, r_all, 1), F32),
                        pltpu.VMEM((n_groups, r_all, HEAD_DIM), F32)])
    split = lambda cache: cache.reshape(n_phys, page, n_groups, hg, HEAD_DIM)
    return pl.pallas_call(
        functools.partial(_diff_sample_body, pps=pps, sub=sub, n_heads=n_heads, hg=hg, page=page, t_new=t,
                          past=n_pages * page, lam_init=lam_init),
        grid_spec=grid_spec,
        out_shape=jax.ShapeDtypeStruct((b, t, hd), BF16),
        compiler_params=_params("arbitrary", "arbitrary"),
        name="diff_attention_sample",
    )(page_table, _alibi_slopes(n_heads), q, k_new, v_new, *lams, subln,
      *([split(cache_k)] * pps), *([split(cache_v)] * pps))


def _compress_body(*refs, n_ops, rows_op, nc, nc_pad, interleaved):
    page_refs = refs[:n_ops]
    pos_ref, w1_ref, w2_ref, o_ref, x_ref = refs[n_ops:]
    c, n_c = pl.program_id(1), pl.num_programs(1)
    blocks_op = rows_op // BLK
    for i, ref in enumerate(page_refs):
        for bi in range(blocks_op):
            blk = (c * n_ops + i) * blocks_op + bi
            r0 = pl.multiple_of(blk * BLK_PITCH, SUBLANES)
            for g in range(NSA_KV):
                if interleaved:
                    rows = ref[0, pl.ds(bi * BLK * NSA_KV + g, BLK, stride=NSA_KV), :]
                else:
                    rows = ref[0, bi * BLK:(bi + 1) * BLK, g * HEAD_DIM:(g + 1) * HEAD_DIM]
                x_ref[g, pl.ds(r0, BLK), :] = rows

    @pl.when(c == n_c - 1)
    def _():
        w1 = w1_ref[...].astype(BF16)
        w2 = w2_ref[...].astype(BF16)
        for g in range(NSA_KV):
            pieces = [(x_ref[g, pl.ds(j, nc, stride=BLK_PITCH), :] + pos_ref[j:j + 1, :]).astype(BF16)
                      for j in range(BLK)]
            a = jnp.concatenate(pieces, axis=1)
            mid = _silu(jnp.dot(a, w1, preferred_element_type=F32))
            out = jnp.dot(mid.astype(BF16), w2, preferred_element_type=F32)
            o_ref[0, g, 0:nc, :] = out
            if nc_pad > nc:
                o_ref[0, g, nc:nc_pad, :] = jnp.zeros((nc_pad - nc, HEAD_DIM), F32)


def compress(page_arrays, page_specs, grid, prefetch, length, pos, w1, w2, nc_pad, interleaved):
    n_ops = len(page_arrays)
    rows_op = length // (grid[1] * n_ops)
    nc = length // BLK
    n_pre = len(prefetch)
    const = lambda shape: pl.BlockSpec(shape, lambda bi, c, *pt: (0,) * len(shape))
    grid_spec = pltpu.PrefetchScalarGridSpec(
        num_scalar_prefetch=n_pre, grid=grid,
        in_specs=list(page_specs) + [const(pos.shape), const(w1.shape), const(w2.shape)],
        out_specs=pl.BlockSpec((1, NSA_KV, nc_pad, HEAD_DIM), lambda bi, c, *pt: (bi, 0, 0, 0)),
        scratch_shapes=[pltpu.VMEM((NSA_KV, nc * BLK_PITCH, HEAD_DIM), F32)])

    def body(*refs):
        _compress_body(*refs[n_pre:], n_ops=n_ops, rows_op=rows_op, nc=nc, nc_pad=nc_pad, interleaved=interleaved)

    return pl.pallas_call(
        body, grid_spec=grid_spec,
        out_shape=jax.ShapeDtypeStruct((grid[0], NSA_KV, nc_pad, HEAD_DIM), F32),
        compiler_params=_params("arbitrary", "arbitrary"),
        name="compress",
    )(*prefetch, *page_arrays, pos, w1, w2)


def compress_prompt(x, pos, w1, w2, nc_pad):
    b, t, w = x.shape
    spec = pl.BlockSpec((1, t, w), lambda bi, c: (bi, 0, 0))
    return compress([x], [spec], (b, 1), [], t, pos, w1, w2, nc_pad, interleaved=False)


def compress_paged(cache, page_table, pos, w1, w2, nc_pad, pages_per_step=32):
    b, n_pages = page_table.shape
    page = cache.shape[1] // NSA_KV
    pps = min(pages_per_step, n_pages)
    assert n_pages % pps == 0

    def spec(i):
        return pl.BlockSpec((1, page * NSA_KV, HEAD_DIM), lambda bi, c, pt: (pt[bi, c * pps + i], 0, 0))

    return compress([cache] * pps, [spec(i) for i in range(pps)], (b, n_pages // pps), [page_table],
                    n_pages * page, pos, w1, w2, nc_pad, interleaved=True)


def _select_blocks(imp, pos_q, n_blocks):
    t, w = imp.shape
    blk = lax.broadcasted_iota(jnp.int32, (t, w), 1)
    complete = blk * BLK + (BLK - 1) <= pos_q
    cur = blk == pos_q // BLK
    val = jnp.where(cur, BIG_SEL, jnp.where(blk == 0, 0.5 * BIG_SEL, jnp.where(complete, imp, NEG)))
    val = jnp.where(blk < n_blocks, val, -3e38)
    rank = jnp.zeros((t, w), jnp.int32)
    for i in range(n_blocks):
        vi = val[:, i:i + 1]
        ahead = (vi > val) | ((vi == val) & (blk > i))
        rank = rank + ahead.astype(jnp.int32)
    return (rank < min(N_SELECT, n_blocks)) & (blk < n_blocks)


def _expand_blocks(sel, first_block, n_keys):
    w = sel.shape[1]
    blk = lax.broadcasted_iota(jnp.int32, (w, n_keys), 0)
    key = lax.broadcasted_iota(jnp.int32, (w, n_keys), 1)
    onehot = jnp.where(blk == first_block + key // BLK, 1.0, 0.0).astype(BF16)
    return jnp.dot(sel, onehot, preferred_element_type=F32)


def _head_rows(q_ref, g, hpg):
    parts = [q_ref[0, :, (g * hpg + j) * HEAD_DIM:(g * hpg + j + 1) * HEAD_DIM] for j in range(hpg)]
    return (jnp.concatenate(parts, axis=0) * (HEAD_DIM ** -0.5 * LOG2E)).astype(BF16)


def _compressed_branch(q, kcb, vcb, slopes, pos_q, nc, hpg):
    t = pos_q.shape[0]
    w = kcb.shape[0]
    blk_end = lax.broadcasted_iota(jnp.int32, (t, w), 1) * BLK + (BLK - 1)
    dist = pos_q - blk_end
    mask = ((dist >= 0) & (blk_end < nc * BLK))[None]
    s = _nt_dot(q, kcb.astype(BF16)).reshape(hpg, t, w) - slopes * dist.astype(F32)[None]
    s = jnp.where(mask, s, NEG)
    e = jnp.where(mask, jnp.exp2(s - jnp.max(s, axis=-1, keepdims=True)), 0.0)
    p = e / jnp.maximum(jnp.sum(e, axis=-1, keepdims=True), 1e-30)
    o = jnp.dot(p.reshape(hpg * t, w).astype(BF16), vcb.astype(BF16), preferred_element_type=F32)
    return o, jnp.sum(p, axis=0)


def _merge_heads(o_ref, gates, g, hpg, o_cmp, o_slc, o_win):
    t = o_cmp.shape[0] // hpg
    for j in range(hpg):
        head = g * hpg + j
        rows = slice(j * t, (j + 1) * t)
        o = (gates[:, 3 * head:3 * head + 1] * o_cmp[rows]
             + gates[:, 3 * head + 1:3 * head + 2] * o_slc[rows]
             + gates[:, 3 * head + 2:3 * head + 3] * o_win[rows])
        o_ref[0, :, head * HEAD_DIM:(head + 1) * HEAD_DIM] = o.astype(o_ref.dtype)


def _nsa_prompt_body(slope_ref, q_ref, kcb_ref, vcb_ref, ks_ref, vs_ref, kw_ref, vw_ref, gate_ref, o_ref,
                     *, tq, ck, wlen, nc, hpg):
    qi = pl.program_id(1)
    q0 = qi * tq
    pos_q = q0 + lax.broadcasted_iota(jnp.int32, (tq, 1), 0)
    gates = gate_ref[0]
    rel_s = pos_q - lax.broadcasted_iota(jnp.int32, (tq, ck), 1)
    col_s = lax.broadcasted_iota(jnp.int32, (1, 1, ck), 2).astype(F32)
    w0 = pl.multiple_of(jnp.maximum(q0 + tq - wlen, 0), tq)
    dist_w = pos_q - (w0 + lax.broadcasted_iota(jnp.int32, (tq, wlen), 1))
    mask_w = ((dist_w >= 0) & (dist_w <= WINDOW))[None]
    pos_w = (w0 - q0 + lax.broadcasted_iota(jnp.int32, (1, 1, wlen), 2)).astype(F32)
    init = _softmax_init((hpg, tq), hpg * tq, HEAD_DIM)
    for g in range(NSA_KV):
        lanes = pl.ds(g * HEAD_DIM, HEAD_DIM)
        q = _head_rows(q_ref, g, hpg)
        slopes = _head_slopes(slope_ref, g * hpg, hpg)
        o_cmp, imp = _compressed_branch(q, kcb_ref[0, g], vcb_ref[0, g], slopes, pos_q, nc, hpg)
        sel = jnp.where(_select_blocks(imp, pos_q, nc), 1.0, 0.0).astype(BF16)

        def slc_chunk(c, carry, causal):
            k0 = pl.multiple_of(c * ck, ck)
            k = ks_ref[0, pl.ds(k0, ck), lanes].astype(BF16)
            v = vs_ref[0, pl.ds(k0, ck), lanes].astype(BF16)
            mask = _expand_blocks(sel, k0 // BLK, ck) > 0.5
            if causal:
                mask = mask & (rel_s >= k0)
            s = _nt_dot(q, k).reshape(hpg, tq, ck) + slopes * (col_s + (k0 - q0).astype(F32))
            return _softmax_update(carry, s, jnp.broadcast_to(mask[None], s.shape), v, zero_masked=False)

        n_full = q0 // ck
        carry = lax.fori_loop(0, n_full, functools.partial(slc_chunk, causal=False), init)
        carry = lax.fori_loop(n_full, (q0 + tq + ck - 1) // ck, functools.partial(slc_chunk, causal=True), carry)
        o_slc = _softmax_finish(carry)

        kw = kw_ref[0, pl.ds(w0, wlen), lanes].astype(BF16)
        vw = vw_ref[0, pl.ds(w0, wlen), lanes].astype(BF16)
        s = _nt_dot(q, kw).reshape(hpg, tq, wlen) + slopes * pos_w
        o_win = _softmax_finish(
            _softmax_update(init, s, jnp.broadcast_to(mask_w, s.shape), vw, zero_masked=False))
        _merge_heads(o_ref, gates, g, hpg, o_cmp, o_slc, o_win)


def nsa_attention_prompt(q, kcb, vcb, ks, vs, kw, vw, gates, tq=128, ck=512):
    b, t, hd = q.shape
    n_heads = hd // HEAD_DIM
    hpg = n_heads // NSA_KV
    tq, ck = min(tq, t), min(ck, t)
    wlen = min(WINDOW + tq, t)
    assert ck % tq == 0 and WINDOW % tq == 0 and N_SELECT >= 2
    tok = pl.BlockSpec((1, tq, hd), lambda bi, qi: (bi, qi, 0))
    cmp_spec = pl.BlockSpec((1,) + kcb.shape[1:], lambda bi, qi: (bi, 0, 0, 0))
    seq = pl.BlockSpec((1, t, NSA_KV * HEAD_DIM), lambda bi, qi: (bi, 0, 0))
    return pl.pallas_call(
        functools.partial(_nsa_prompt_body, tq=tq, ck=ck, wlen=wlen, nc=t // BLK, hpg=hpg),
        grid=(b, t // tq),
        in_specs=[_smem_spec(), tok, cmp_spec, cmp_spec, seq, seq, seq, seq,
                  pl.BlockSpec((1, tq, LANES), lambda bi, qi: (bi, qi, 0))],
        out_specs=tok,
        out_shape=jax.ShapeDtypeStruct((b, t, hd), BF16),
        compiler_params=_params("arbitrary", "arbitrary"),
        name="nsa_attention_prompt",
    )(_alibi_slopes(n_heads), q, kcb, vcb, ks, vs, kw, vw, gates)


def _nsa_sample_body(pt_ref, slope_ref, q_ref, kcb_ref, vcb_ref, ksn_ref, vsn_ref, kwn_ref, vwn_ref,
                     wk_ref, wv_ref, gate_ref, *rest, pps, page, t_new, past, hpg, sel_w):
    k_pages = rest[:pps]
    v_pages = rest[pps:2 * pps]
    o_ref = rest[2 * pps]
    q_s, ocmp_s, sel_s, m_s, l_s, acc_s = rest[2 * pps + 1:]
    c, n_c = pl.program_id(1), pl.num_programs(1)
    rows = hpg * t_new
    nc = past // BLK
    n_blocks = -(-(past + t_new) // BLK)
    pos_q = past + lax.broadcasted_iota(jnp.int32, (t_new, 1), 0)
    keys_step = pps * page

    @pl.when(c == 0)
    def _():
        for g in range(NSA_KV):
            q = _head_rows(q_ref, g, hpg)
            q_s[g] = q
            slopes = _head_slopes(slope_ref, g * hpg, hpg)
            o_cmp, imp = _compressed_branch(q, kcb_ref[0, g], vcb_ref[0, g], slopes, pos_q, nc, hpg)
            ocmp_s[g] = o_cmp
            imp = jnp.concatenate([imp, jnp.zeros((t_new, sel_w - imp.shape[1]), F32)], axis=1)
            sel_s[g] = jnp.where(_select_blocks(imp, pos_q, n_blocks), 1.0, 0.0)
        m_s[...] = jnp.full(m_s.shape, NEG, F32)
        l_s[...] = jnp.zeros(l_s.shape, F32)
        acc_s[...] = jnp.zeros(acc_s.shape, F32)

    def group_rows(ref, g):
        return ref[0, pl.ds(g, page, stride=NSA_KV), :]

    rel = pos_q - lax.broadcasted_iota(jnp.int32, (t_new, keys_step), 1)
    for g in range(NSA_KV):
        slopes = _head_slopes(slope_ref, g * hpg, hpg)
        q = q_s[g]
        k0 = c * keys_step
        dist = rel - k0
        mask = (_expand_blocks(sel_s[g].astype(BF16), k0 // BLK, keys_step) > 0.5)[None]
        k = jnp.concatenate([group_rows(k_pages[i], g) for i in range(pps)], axis=0).astype(BF16)
        v = jnp.concatenate([group_rows(v_pages[i], g) for i in range(pps)], axis=0).astype(BF16)
        s = _nt_dot(q, k).reshape(hpg, t_new, keys_step) - slopes * dist.astype(F32)[None]
        carry = _softmax_update((m_s[g], l_s[g], acc_s[g]), s, jnp.broadcast_to(mask, s.shape), v)
        m_s[g], l_s[g], acc_s[g] = carry

    @pl.when(c == n_c - 1)
    def _():
        gates = gate_ref[0]
        pad = jnp.zeros((page - t_new, HEAD_DIM), F32)
        col = lax.broadcasted_iota(jnp.int32, (t_new, page), 1)
        dist_new = (pos_q - past) - col
        for g in range(NSA_KV):
            lanes = pl.ds(g * HEAD_DIM, HEAD_DIM)
            slopes = _head_slopes(slope_ref, g * hpg, hpg)
            q = q_s[g]
            k = jnp.concatenate([ksn_ref[0, :, lanes], pad], axis=0).astype(BF16)
            v = jnp.concatenate([vsn_ref[0, :, lanes], pad], axis=0).astype(BF16)
            sel_new = _expand_blocks(sel_s[g].astype(BF16), nc, page) > 0.5
            mask = (sel_new & (dist_new >= 0) & (col < t_new))[None]
            s = _nt_dot(q, k).reshape(hpg, t_new, page) - slopes * dist_new.astype(F32)[None]
            carry = _softmax_update((m_s[g], l_s[g], acc_s[g]), s, jnp.broadcast_to(mask, s.shape), v)
            o_slc = _softmax_finish(carry)
            w_buf = wk_ref.shape[1] // NSA_KV
            win_rows = pl.ds(g, w_buf, stride=NSA_KV)
            colw = lax.broadcasted_iota(jnp.int32, (t_new, w_buf), 1)
            dist_w = pos_q - (past - w_buf + colw)
            carry = _softmax_init((hpg, t_new), rows, HEAD_DIM)
            sw = _nt_dot(q, wk_ref[0, win_rows, :].astype(BF16)).reshape(hpg, t_new, w_buf)
            sw = sw - slopes * dist_w.astype(F32)[None]
            mask_w = ((dist_w >= 0) & (dist_w <= WINDOW))[None]
            carry = _softmax_update(carry, sw, jnp.broadcast_to(mask_w, sw.shape),
                                    wv_ref[0, win_rows, :].astype(BF16))
            k = jnp.concatenate([kwn_ref[0, :, lanes], pad], axis=0).astype(BF16)
            v = jnp.concatenate([vwn_ref[0, :, lanes], pad], axis=0).astype(BF16)
            mask_n = ((dist_new >= 0) & (dist_new <= WINDOW) & (col < t_new))[None]
            sn = _nt_dot(q, k).reshape(hpg, t_new, page) - slopes * dist_new.astype(F32)[None]
            carry = _softmax_update(carry, sn, jnp.broadcast_to(mask_n, sn.shape), v)
            o_win = _softmax_finish(carry)
            _merge_heads(o_ref, gates, g, hpg, ocmp_s[g], o_slc, o_win)


def nsa_attention_sample(q, kcb, vcb, ks_new, vs_new, kw_new, vw_new, win_k, win_v, gates,
                         cache_k, cache_v, page_table, pages_per_step=16):
    b, t, hd = q.shape
    n_heads = hd // HEAD_DIM
    hpg = n_heads // NSA_KV
    n_pages = page_table.shape[1]
    page = cache_k.shape[1] // NSA_KV
    past = n_pages * page
    pps = min(pages_per_step, n_pages)
    n_blocks = -(-(past + t) // BLK)
    sel_w = -(-n_blocks // LANES) * LANES
    assert n_pages % pps == 0 and t <= page and page % BLK == 0 and kcb.shape[2] * BLK >= past
    width = NSA_KV * HEAD_DIM
    tok = pl.BlockSpec((1, t, hd), lambda bi, c, pt: (bi, 0, 0))
    new = pl.BlockSpec((1, t, width), lambda bi, c, pt: (bi, 0, 0))
    cmp_spec = pl.BlockSpec((1,) + kcb.shape[1:], lambda bi, c, pt: (bi, 0, 0, 0))
    win = pl.BlockSpec((1,) + win_k.shape[1:], lambda bi, c, pt: (bi, 0, 0))

    def page_spec(i):
        return pl.BlockSpec((1, page * NSA_KV, HEAD_DIM), lambda bi, c, pt: (pt[bi, c * pps + i], 0, 0))

    pages = [page_spec(i) for i in range(pps)]
    rows = hpg * t
    grid_spec = pltpu.PrefetchScalarGridSpec(
        num_scalar_prefetch=1,
        grid=(b, n_pages // pps),
        in_specs=[_smem_spec(), tok, cmp_spec, cmp_spec, new, new, new, new, win, win,
                  pl.BlockSpec((1, t, LANES), lambda bi, c, pt: (bi, 0, 0))] + pages + pages,
        out_specs=tok,
        scratch_shapes=[pltpu.VMEM((NSA_KV, rows, HEAD_DIM), BF16),
                        pltpu.VMEM((NSA_KV, rows, HEAD_DIM), F32),
                        pltpu.VMEM((NSA_KV, t, sel_w), F32),
                        pltpu.VMEM((NSA_KV, hpg, t, 1), F32),
                        pltpu.VMEM((NSA_KV, hpg, t, 1), F32),
                        pltpu.VMEM((NSA_KV, rows, HEAD_DIM), F32)])
    return pl.pallas_call(
        functools.partial(_nsa_sample_body, pps=pps, page=page, t_new=t, past=past, hpg=hpg, sel_w=sel_w),
        grid_spec=grid_spec,
        out_shape=jax.ShapeDtypeStruct((b, t, hd), BF16),
        compiler_params=_params("arbitrary", "arbitrary"),
        name="nsa_attention_sample",
    )(page_table, _alibi_slopes(n_heads), q, kcb, vcb, ks_new, vs_new, kw_new, vw_new, win_k, win_v, gates,
      *([cache_k] * pps), *([cache_v] * pps))


def _last_rows(a, n):
    a = jnp.pad(a, ((0, 0), (n, 0)) + ((0, 0),) * (a.ndim - 2))
    return a[:, a.shape[1] - n:]


def _project(x, mods, norm_g, w_in_t, mix_diff, mix_nsa):
    b, t, d = x.shape
    h = modulate(x, mods[0], mods[1], norm_g)
    seg = lambda col0, n: matmul(h, w_in_t, col0=col0, ncols=n, wt=True).reshape(b, t, n)
    dq, dk, dv = seg(0, mix_diff), seg(mix_diff, mix_diff), seg(2 * mix_diff, mix_diff)
    nq = seg(3 * mix_diff, mix_nsa)
    n_gate = 3 * (mix_nsa // HEAD_DIM)
    outs = kv_gate_projection(h, w_in_t, 3 * mix_diff + mix_nsa, 6, n_gate)
    kvs = [o.reshape(b, t, NSA_KV * HEAD_DIM) for o in outs[:6]]
    gates = outs[6].reshape(b, t, LANES)
    return dq, dk, dv, nq, kvs, gates


def _finish(x, o_diff, o_nsa, mods, norm2_g, w_out, w_up, w_down, final_g):
    b, t, d = x.shape
    mix = matmul_split(o_diff.reshape(b * t, -1), o_nsa.reshape(b * t, -1), w_out)
    x1, h2 = residual_modulate(x, mix, mods[2], mods[3], mods[4], norm2_g)
    u = matmul(h2, w_up, out_dtype=BF16, relu2=True)
    ffn = matmul(u, w_down, tn=1024, tk=2048)
    return residual_final_norm(x1, ffn, mods[5], final_g)


def kernel(x_prompt, x_sample, c_prompt, c_sample, cache_diff_k, cache_diff_v, cache_cmp_k, cache_cmp_v, cache_slc_k, cache_slc_v, state_win_k, state_win_v, page_table, norm1_g, norm2_g, ada_w, ada_b, w_in, w_out, diff_lq1, diff_lk1, diff_lq2, diff_lk2, diff_subln_g, cmp_k_pos, cmp_k_w1, cmp_k_w2, cmp_v_pos, cmp_v_w1, cmp_v_w2, w_up, w_down, final_g):
    depth = ada_w.shape[0]
    assert depth == 1
    l = 0
    bp, tp, d = x_prompt.shape
    bs, ts, _ = x_sample.shape
    n_phys, page, n_heads_diff, _ = cache_diff_k.shape[1:]
    mix_diff = n_heads_diff * HEAD_DIM
    mix_nsa = d - mix_diff
    w_buf = state_win_k.shape[2]
    lam_init = 0.8 - 0.6 * math.exp(-0.3 * l)
    lams = [a[l].reshape(1, DIFF_HALF) for a in (diff_lq1, diff_lk1, diff_lq2, diff_lk2)]
    subln = diff_subln_g[l].reshape(1, HEAD_DIM)

    n_c = bp + bs
    n_c_pad = -(-n_c // 16) * 16
    c_all = jnp.concatenate([c_prompt, c_sample, jnp.zeros((n_c_pad - n_c, d), F32)], axis=0)
    mod = ada_modulation(c_all, ada_w[l], ada_b[l]).reshape(n_c_pad, 6, 1, d)
    mods_p = [mod[:bp, i] for i in range(6)]
    mods_s = [mod[bp:n_c, i] for i in range(6)]

    w_in_t = jnp.swapaxes(w_in[l], 0, 1)

    dq, dk_p, dv_p, nq, kvs_p, gates = _project(x_prompt, mods_p, norm1_g[l], w_in_t, mix_diff, mix_nsa)
    o_diff = diff_attention_prompt(dq, dk_p, dv_p, lams, subln, lam_init)
    nc_pad = -(-max(tp, page_table.shape[1] * page) // (BLK * LANES)) * LANES
    kcb = compress_prompt(kvs_p[0], cmp_k_pos[l], cmp_k_w1[l], cmp_k_w2[l], nc_pad)
    vcb = compress_prompt(kvs_p[1], cmp_v_pos[l], cmp_v_w1[l], cmp_v_w2[l], nc_pad)
    o_nsa = nsa_attention_prompt(nq, kcb, vcb, kvs_p[2], kvs_p[3], kvs_p[4], kvs_p[5], gates)
    y_prompt = _finish(x_prompt, o_diff, o_nsa, mods_p, norm2_g[l], w_out[l], w_up[l], w_down[l], final_g)

    dq, dk_s, dv_s, nq, kvs_s, gates = _project(x_sample, mods_s, norm1_g[l], w_in_t, mix_diff, mix_nsa)
    o_diff = diff_attention_sample(dq, dk_s, dv_s, cache_diff_k[l], cache_diff_v[l], page_table, lams, subln, lam_init)
    flat_nsa = lambda c: c[l].reshape(n_phys, page * NSA_KV, HEAD_DIM)
    kcb = compress_paged(flat_nsa(cache_cmp_k), page_table, cmp_k_pos[l], cmp_k_w1[l], cmp_k_w2[l], nc_pad)
    vcb = compress_paged(flat_nsa(cache_cmp_v), page_table, cmp_v_pos[l], cmp_v_w1[l], cmp_v_w2[l], nc_pad)
    win_k = state_win_k[l].reshape(bs, w_buf * NSA_KV, HEAD_DIM)
    win_v = state_win_v[l].reshape(bs, w_buf * NSA_KV, HEAD_DIM)
    o_nsa = nsa_attention_sample(nq, kcb, vcb, kvs_s[2], kvs_s[3], kvs_s[4], kvs_s[5], win_k, win_v, gates,
                                 flat_nsa(cache_slc_k), flat_nsa(cache_slc_v), page_table)
    y_sample = _finish(x_sample, o_diff, o_nsa, mods_s, norm2_g[l], w_out[l], w_up[l], w_down[l], final_g)

    heads = lambda a, b, t, h: a.reshape(1, b, t, h, HEAD_DIM)
    out_p = [heads(dk_p, bp, tp, n_heads_diff), heads(dv_p, bp, tp, n_heads_diff)]
    out_p += [heads(a, bp, tp, NSA_KV) for a in kvs_p[:4]]
    out_p += [heads(_last_rows(a, w_buf), bp, w_buf, NSA_KV) for a in kvs_p[4:]]
    out_s = [heads(dk_s, bs, ts, n_heads_diff), heads(dv_s, bs, ts, n_heads_diff)]
    out_s += [heads(a, bs, ts, NSA_KV) for a in kvs_s[:4]]
    out_s += [jnp.concatenate([w[l], a.reshape(bs, ts, NSA_KV, HEAD_DIM)], axis=1)[None, :, ts:]
              for w, a in ((state_win_k, kvs_s[4]), (state_win_v, kvs_s[5]))]
    return (y_prompt, y_sample, *out_p, *out_s)
```

```python
import functools
import math

import numpy as np
import jax
import jax.numpy as jnp
from jax import lax
from jax.experimental import pallas as pl
from jax.experimental.pallas import tpu as pltpu

F32 = jnp.float32
BF16 = jnp.bfloat16

DIFF_HALF = 64
HEAD_DIM = 128
NSA_KV = 2
BLK = 64
N_SELECT = 16
WINDOW = 512
EPS = 1e-6
NEG = -1e30
BIG_SEL = 1e4
LOG2E = math.log2(math.e)
LANES = 128
SUBLANES = 8
BLK_PITCH = BLK + SUBLANES
VMEM_LIMIT_BYTES = 56 * 1024 * 1024


def _params(*sem):
    return pltpu.CompilerParams(dimension_semantics=sem, vmem_limit_bytes=VMEM_LIMIT_BYTES)


def _alibi_slopes(n):
    return jnp.asarray(np.exp2(-8.0 * np.arange(1, n + 1) / n), dtype=F32)


def _silu(x):
    return x * jax.nn.sigmoid(x)


def _smem_spec():
    return pl.BlockSpec(memory_space=pltpu.SMEM)


def _ada_body(c_ref, w_ref, b_ref, o_ref):
    a = _silu(c_ref[...]).astype(BF16)
    o_ref[...] = jnp.dot(a, w_ref[...].astype(BF16), preferred_element_type=F32) + b_ref[...]


def ada_modulation(c, w, b, tn=512):
    m, k = c.shape
    n = w.shape[1]
    tn = min(tn, n)
    return pl.pallas_call(
        _ada_body,
        grid=(n // tn,),
        in_specs=[pl.BlockSpec((m, k), lambda j: (0, 0)),
                  pl.BlockSpec((k, tn), lambda j: (0, j)),
                  pl.BlockSpec((1, tn), lambda j: (0, j))],
        out_specs=pl.BlockSpec((m, tn), lambda j: (0, j)),
        out_shape=jax.ShapeDtypeStruct((m, n), F32),
        compiler_params=_params("arbitrary"),
        name="ada_modulation",
    )(c, w, b.reshape(1, n))


def _rms(x, g):
    return x * lax.rsqrt(jnp.mean(x * x, axis=-1, keepdims=True) + EPS) * g


def _modulate_body(x_ref, sh_ref, sc_ref, g_ref, h_ref):
    h = _rms(x_ref[...], g_ref[...]) * (1.0 + sc_ref[...]) + sh_ref[...]
    h_ref[...] = h.reshape(h_ref.shape).astype(h_ref.dtype)


def _mid_body(x_ref, mix_ref, g1_ref, sh_ref, sc_ref, g_ref, x1_ref, h_ref):
    x1 = x_ref[...] + g1_ref[...] * mix_ref[...].reshape(x_ref.shape)
    x1_ref[...] = x1
    h = _rms(x1, g_ref[...]) * (1.0 + sc_ref[...]) + sh_ref[...]
    h_ref[...] = h.reshape(h_ref.shape).astype(h_ref.dtype)


def _final_body(x_ref, ffn_ref, g2_ref, g_ref, y_ref):
    x2 = x_ref[...] + g2_ref[...] * ffn_ref[...].reshape(x_ref.shape)
    y_ref[...] = _rms(x2, g_ref[...])


def _row_tiles(b, t):
    if t >= 256:
        return 1, 256
    return b, t


def _row_specs(b, t, d):
    bb, tt = _row_tiles(b, t)
    nt = t // tt
    grid = (b // bb, nt)
    x3 = pl.BlockSpec((bb, tt, d), lambda i, j: (i, j, 0))
    mod = pl.BlockSpec((bb, 1, d), lambda i, j: (i, 0, 0))
    gain = pl.BlockSpec((1, 1, d), lambda i, j: (0, 0, 0))
    flat = pl.BlockSpec((bb * tt, d), lambda i, j: (i * nt + j, 0))
    return grid, x3, mod, gain, flat


def modulate(x, shift, scale, g):
    b, t, d = x.shape
    grid, x3, mod, gain, flat = _row_specs(b, t, d)
    return pl.pallas_call(
        _modulate_body, grid=grid,
        in_specs=[x3, mod, mod, gain], out_specs=flat,
        out_shape=jax.ShapeDtypeStruct((b * t, d), BF16),
        compiler_params=_params("arbitrary", "arbitrary"),
        name="modulate",
    )(x, shift, scale, g.reshape(1, 1, d))


def residual_modulate(x, mix, gate1, shift, scale, g):
    b, t, d = x.shape
    grid, x3, mod, gain, flat = _row_specs(b, t, d)
    return pl.pallas_call(
        _mid_body, grid=grid,
        in_specs=[x3, flat, mod, mod, mod, gain], out_specs=[x3, flat],
        out_shape=[jax.ShapeDtypeStruct((b, t, d), F32), jax.ShapeDtypeStruct((b * t, d), BF16)],
        compiler_params=_params("arbitrary", "arbitrary"),
        name="residual_modulate",
    )(x, mix, gate1, shift, scale, g.reshape(1, 1, d))


def residual_final_norm(x, ffn, gate2, g):
    b, t, d = x.shape
    grid, x3, mod, gain, flat = _row_specs(b, t, d)
    return pl.pallas_call(
        _final_body, grid=grid,
        in_specs=[x3, flat, mod, gain], out_specs=x3,
        out_shape=jax.ShapeDtypeStruct((b, t, d), F32),
        compiler_params=_params("arbitrary", "arbitrary"),
        name="residual_final_norm",
    )(x, ffn, gate2, g.reshape(1, 1, d))


def _nt_dot(a, b):
    return lax.dot_general(a, b, (((1,), (1,)), ((), ())), preferred_element_type=F32)


def _w_dot(a, w, w_rows_are_outputs):
    w = w.astype(BF16)
    return _nt_dot(a, w) if w_rows_are_outputs else jnp.dot(a, w, preferred_element_type=F32)


def _mm_body(a_ref, w_ref, o_ref, *scratch, nk, relu2, wt):
    part = _w_dot(a_ref[...], w_ref[...], wt)

    def finish(r):
        if relu2:
            r = jnp.square(jnp.maximum(r, 0.0))
        o_ref[...] = r.astype(o_ref.dtype)

    if nk == 1:
        finish(part)
    else:
        acc_ref, = scratch
        k = pl.program_id(2)

        @pl.when(k == 0)
        def _():
            acc_ref[...] = part

        @pl.when(k > 0)
        def _():
            acc_ref[...] += part

        @pl.when(k == nk - 1)
        def _():
            finish(acc_ref[...])


def matmul(a, w, *, col0=0, ncols=None, tm=1024, tn=512, tk=None, out_dtype=F32, relu2=False, wt=False):
    m, k = a.shape
    n_total = w.shape[0] if wt else w.shape[1]
    ncols = n_total - col0 if ncols is None else ncols
    tm, tn = min(tm, m), min(tn, ncols)
    tk = k if tk is None else min(tk, k)
    assert m % tm == 0 and ncols % tn == 0 and k % tk == 0 and col0 % tn == 0
    nk, jb = k // tk, col0 // tn
    if wt:
        w_spec = pl.BlockSpec((tn, tk), lambda i, j, kk: (jb + j, kk))
    else:
        w_spec = pl.BlockSpec((tk, tn), lambda i, j, kk: (kk, jb + j))
    return pl.pallas_call(
        functools.partial(_mm_body, nk=nk, relu2=relu2, wt=wt),
        grid=(m // tm, ncols // tn, nk),
        in_specs=[pl.BlockSpec((tm, tk), lambda i, j, kk: (i, kk)), w_spec],
        out_specs=pl.BlockSpec((tm, tn), lambda i, j, kk: (i, j)),
        out_shape=jax.ShapeDtypeStruct((m, ncols), out_dtype),
        scratch_shapes=[pltpu.VMEM((tm, tn), F32)] if nk > 1 else [],
        compiler_params=_params("arbitrary", "arbitrary", "arbitrary"),
        name="matmul",
    )(a, w)


def _mm_split_body(a0_ref, a1_ref, w_ref, o_ref):
    k0 = a0_ref.shape[1]
    o_ref[...] = (_w_dot(a0_ref[...], w_ref[:k0, :], False) + _w_dot(a1_ref[...], w_ref[k0:, :], False))


def matmul_split(a0, a1, w, tm=1024, tn=512):
    m, k0 = a0.shape
    k1, n = a1.shape[1], w.shape[1]
    tm, tn = min(tm, m), min(tn, n)
    assert m % tm == 0 and n % tn == 0 and w.shape[0] == k0 + k1
    return pl.pallas_call(
        _mm_split_body,
        grid=(m // tm, n // tn),
        in_specs=[pl.BlockSpec((tm, k0), lambda i, j: (i, 0)),
                  pl.BlockSpec((tm, k1), lambda i, j: (i, 0)),
                  pl.BlockSpec((k0 + k1, tn), lambda i, j: (0, j))],
        out_specs=pl.BlockSpec((tm, tn), lambda i, j: (i, j)),
        out_shape=jax.ShapeDtypeStruct((m, n), F32),
        compiler_params=_params("arbitrary", "arbitrary"),
        name="matmul_split",
    )(a0, a1, w)


def _kv_gate_body(a_ref, w_ref, *o_refs, n_kv, n_gate):
    j = pl.program_id(1)
    r = _w_dot(a_ref[...], w_ref[...], True)
    for n in range(n_kv):
        @pl.when(j == n)
        def _(n=n):
            o_refs[n][...] = r

    @pl.when(j == n_kv)
    def _():
        g = r[:, :LANES]
        col = lax.broadcasted_iota(jnp.int32, g.shape, 1)
        o_refs[n_kv][...] = jnp.where(col < n_gate, jax.nn.sigmoid(g), 0.0)


def kv_gate_projection(a, w, col0, n_kv, n_gate, tm=1024):
    m, k = a.shape
    tm = min(tm, m)
    width = 2 * LANES
    assert col0 % width == 0 and n_gate <= LANES
    jb = col0 // width
    kv = pl.BlockSpec((tm, width), lambda i, j: (i, 0))
    return pl.pallas_call(
        functools.partial(_kv_gate_body, n_kv=n_kv, n_gate=n_gate),
        grid=(m // tm, n_kv + 1),
        in_specs=[pl.BlockSpec((tm, k), lambda i, j: (i, 0)),
                  pl.BlockSpec((width, k), lambda i, j: (jb + j, 0))],
        out_specs=[kv] * n_kv + [pl.BlockSpec((tm, LANES), lambda i, j: (i, 0))],
        out_shape=[jax.ShapeDtypeStruct((m, 2 * LANES), F32)] * n_kv + [jax.ShapeDtypeStruct((m, LANES), F32)],
        compiler_params=_params("arbitrary", "arbitrary"),
        name="kv_gate_projection",
    )(a, w)


def _softmax_update(carry, s, mask, v, zero_masked=True):
    m, l, acc = carry
    if mask is not None:
        s = jnp.where(mask, s, NEG)
    m_new = jnp.maximum(m, jnp.max(s, axis=-1, keepdims=True))
    p = jnp.exp2(s - m_new)
    if mask is not None and zero_masked:
        p = jnp.where(mask, p, 0.0)
    alpha = jnp.exp2(m - m_new)
    l = alpha * l + jnp.sum(p, axis=-1, keepdims=True)
    rows = acc.shape[0]
    pv = jnp.dot(p.reshape(rows, p.shape[-1]).astype(BF16), v, preferred_element_type=F32)
    acc = alpha.reshape(rows, 1) * acc + pv
    return m_new, l, acc


def _softmax_init(lead, rows_total, d):
    return (jnp.full(lead + (1,), NEG, F32), jnp.zeros(lead + (1,), F32), jnp.zeros((rows_total, d), F32))


def _softmax_finish(carry):
    _, l, acc = carry
    return acc / jnp.maximum(l.reshape(acc.shape[0], 1), 1e-30)


def _lambda_full(lq1, lk1, lq2, lk2, lam_init):
    a = jnp.sum(lq1 * lk1, axis=-1, keepdims=True)
    b = jnp.sum(lq2 * lk2, axis=-1, keepdims=True)
    return jnp.exp(a) - jnp.exp(b) + lam_init


def _diff_qcat(q):
    lane = lax.broadcasted_iota(jnp.int32, q.shape, 1)
    qs = q * (DIFF_HALF ** -0.5 * LOG2E)
    first = jnp.where(lane < DIFF_HALF, qs, 0.0)
    second = jnp.where(lane >= DIFF_HALF, qs, 0.0)
    return jnp.concatenate([first, second], axis=0).astype(BF16)


def _diff_finish(o12, lam, lam_init, subln):
    t = o12.shape[0] // 2
    o = o12[:t] - lam * o12[t:]
    return _rms(o, subln) * (1.0 - lam_init)


def _diff_prompt_body(slope_ref, q_ref, k_ref, v_ref, lq1, lk1, lq2, lk2, subln_ref, o_ref, *, tq, ck, lam_init):
    h, qi = pl.program_id(1), pl.program_id(2)
    slope = slope_ref[h] * LOG2E
    qcat = _diff_qcat(q_ref[0])
    q0 = qi * tq
    row = lax.broadcasted_iota(jnp.int32, (2 * tq, 1), 0)
    pos_q = q0 + jnp.where(row >= tq, row - tq, row)
    col = lax.broadcasted_iota(jnp.int32, (1, ck), 1)
    colf = col.astype(F32)

    def chunk(c, carry, causal):
        k0 = pl.multiple_of(c * ck, ck)
        k = k_ref[0, pl.ds(k0, ck), :].astype(BF16)
        v = v_ref[0, pl.ds(k0, ck), :].astype(BF16)
        s = _nt_dot(qcat, k) + slope * (colf + (k0 - q0).astype(F32))
        return _softmax_update(carry, s, (pos_q >= k0 + col) if causal else None, v, zero_masked=False)

    n_full = q0 // ck
    n_chunks = (q0 + tq + ck - 1) // ck
    carry = _softmax_init((2 * tq,), 2 * tq, HEAD_DIM)
    carry = lax.fori_loop(0, n_full, functools.partial(chunk, causal=False), carry)
    carry = lax.fori_loop(n_full, n_chunks, functools.partial(chunk, causal=True), carry)
    lam = _lambda_full(lq1[...], lk1[...], lq2[...], lk2[...], lam_init)
    o_ref[0] = _diff_finish(_softmax_finish(carry), lam, lam_init, subln_ref[...]).astype(o_ref.dtype)


def diff_attention_prompt(q, k, v, lams, subln, lam_init, tq=512, ck=512):
    b, t, hd = q.shape
    n_heads = hd // HEAD_DIM
    tq, ck = min(tq, t), min(ck, t)
    assert ck % tq == 0
    qspec = pl.BlockSpec((1, tq, HEAD_DIM), lambda bi, h, qi: (bi, qi, h))
    kvspec = pl.BlockSpec((1, t, HEAD_DIM), lambda bi, h, qi: (bi, 0, h))
    vec = pl.BlockSpec((1, DIFF_HALF), lambda bi, h, qi: (0, 0))
    return pl.pallas_call(
        functools.partial(_diff_prompt_body, tq=tq, ck=ck, lam_init=lam_init),
        grid=(b, n_heads, t // tq),
        in_specs=[_smem_spec(), qspec, kvspec, kvspec, vec, vec, vec, vec,
                  pl.BlockSpec((1, HEAD_DIM), lambda bi, h, qi: (0, 0))],
        out_specs=qspec,
        out_shape=jax.ShapeDtypeStruct((b, t, hd), BF16),
        compiler_params=_params("arbitrary", "arbitrary", "arbitrary"),
        name="diff_attention_prompt",
    )(_alibi_slopes(n_heads), q, k, v, *lams, subln)


def _head_slopes(slope_ref, first, n):
    j = lax.broadcasted_iota(jnp.int32, (n, 1, 1), 0)
    out = jnp.zeros((n, 1, 1), F32)
    for jj in range(n):
        out = jnp.where(j == jj, slope_ref[first + jj] * LOG2E, out)
    return out


def _diff_sample_body(pt_ref, slope_ref, q_ref, kn_ref, vn_ref, lq1, lk1, lq2, lk2, subln_ref, *rest,
                      pps, sub, n_heads, hg, page, t_new, past, lam_init):
    k_pages = rest[:pps]
    v_pages = rest[pps:2 * pps]
    o_ref = rest[2 * pps]
    qcat_ref, bias_ref, srow_ref, m_ref, l_ref, acc_ref = rest[2 * pps + 1:]
    c, n_c = pl.program_id(1), pl.num_programs(1)
    rows = 2 * t_new
    n_groups = n_heads // hg
    r_all, cols = hg * rows, page * hg

    @pl.when(c == 0)
    def _():
        row = lax.broadcasted_iota(jnp.int32, (r_all, cols), 0)
        col = lax.broadcasted_iota(jnp.int32, (r_all, cols), 1)
        row1 = lax.broadcasted_iota(jnp.int32, (r_all, 1), 0)
        same_head = (col % hg) == (row // rows)
        key = (col // hg).astype(F32)
        for gi in range(n_groups):
            heads = range(gi * hg, (gi + 1) * hg)
            qcat_ref[gi] = jnp.concatenate(
                [_diff_qcat(q_ref[0, :, h * HEAD_DIM:(h + 1) * HEAD_DIM]) for h in heads], axis=0)
            srow = jnp.zeros((r_all, 1), F32)
            for j, h in enumerate(heads):
                srow = jnp.where(row1 // rows == j, slope_ref[h] * LOG2E, srow)
            srow_ref[gi] = srow
            bias_ref[gi] = jnp.where(same_head, srow * key, NEG)
        m_ref[...] = jnp.full(m_ref.shape, NEG, F32)
        l_ref[...] = jnp.zeros(l_ref.shape, F32)
        acc_ref[...] = jnp.zeros(acc_ref.shape, F32)

    def page_rows(ref, gi):
        return ref[0, :, gi].reshape(cols, HEAD_DIM).astype(BF16)

    for i0 in range(0, pps, sub):
        pages_now = range(i0, i0 + sub)
        scores = []
        for gi in range(n_groups):
            parts = []
            for i in pages_now:
                base = ((c * pps + i) * page - past).astype(F32)
                parts.append(_nt_dot(qcat_ref[gi], page_rows(k_pages[i], gi)) + bias_ref[gi] + srow_ref[gi] * base)
            scores.append(jnp.concatenate(parts, axis=1))
        for gi in range(n_groups):
            v = jnp.concatenate([page_rows(v_pages[i], gi) for i in pages_now], axis=0)
            m_ref[gi], l_ref[gi], acc_ref[gi] = _softmax_update(
                (m_ref[gi], l_ref[gi], acc_ref[gi]), scores[gi], None, v)

    @pl.when(c == n_c - 1)
    def _():
        lam = _lambda_full(lq1[...], lk1[...], lq2[...], lk2[...], lam_init)
        pad = jnp.zeros((page - t_new, HEAD_DIM), F32)
        r = lax.broadcasted_iota(jnp.int32, (rows, page), 0)
        cc = lax.broadcasted_iota(jnp.int32, (rows, page), 1)
        visible = (jnp.where(r >= t_new, r - t_new, r) >= cc) & (cc < t_new)
        for h in range(n_heads):
            gi, j = divmod(h, hg)
            rs = slice(j * rows, (j + 1) * rows)
            lanes = slice(h * HEAD_DIM, (h + 1) * HEAD_DIM)
            k = jnp.concatenate([kn_ref[0, :, lanes], pad], axis=0).astype(BF16)
            v = jnp.concatenate([vn_ref[0, :, lanes], pad], axis=0).astype(BF16)
            sn = _nt_dot(qcat_ref[gi, rs, :], k) + (slope_ref[h] * LOG2E) * cc.astype(F32)
            carry = _softmax_update((m_ref[gi, rs, :], l_ref[gi, rs, :], acc_ref[gi, rs, :]), sn, visible, v)
            o = _diff_finish(_softmax_finish(carry), lam, lam_init, subln_ref[...])
            o_ref[0, :, lanes] = o.astype(o_ref.dtype)


def diff_attention_sample(q, k_new, v_new, cache_k, cache_v, page_table, lams, subln, lam_init,
                          pages_per_step=8, pages_per_softmax_step=4):
    b, t, hd = q.shape
    n_heads = hd // HEAD_DIM
    n_phys, page = cache_k.shape[:2]
    n_pages = page_table.shape[1]
    hg = math.gcd(n_heads, SUBLANES)
    n_groups = n_heads // hg
    pps = min(pages_per_step, n_pages)
    sub = min(pages_per_softmax_step, pps)
    assert n_pages % pps == 0 and pps % sub == 0 and t <= page
    tok = pl.BlockSpec((1, t, hd), lambda bi, c, pt: (bi, 0, 0))
    vec = pl.BlockSpec((1, DIFF_HALF), lambda bi, c, pt: (0, 0))

    def page_spec(i):
        return pl.BlockSpec((1, page, n_groups, hg, HEAD_DIM), lambda bi, c, pt: (pt[bi, c * pps + i], 0, 0, 0, 0))

    pages = [page_spec(i) for i in range(pps)]
    r_all = hg * 2 * t
    grid_spec = pltpu.PrefetchScalarGridSpec(
        num_scalar_prefetch=1,
        grid=(b, n_pages // pps),
        in_specs=[_smem_spec(), tok, tok, tok, vec, vec, vec, vec,
                  pl.BlockSpec((1, HEAD_DIM), lambda bi, c, pt: (0, 0))] + pages + pages,
        out_specs=tok,
        scratch_shapes=[pltpu.VMEM((n_groups, r_all, HEAD_DIM), BF16),
                        pltpu.VMEM((n_groups, r_all, page * hg), F32),
                        pltpu.VMEM((n_groups, r_all, 1), F32),
                        pltpu.VMEM((n_groups, r_all, 1), F32),
                        pltpu.VMEM((n_groups, r_all, 1), F32),
                        pltpu.VMEM((n_groups, r_all, HEAD_DIM), F32)])
    split = lambda cache: cache.reshape(n_phys, page, n_groups, hg, HEAD_DIM)
    return pl.pallas_call(
        functools.partial(_diff_sample_body, pps=pps, sub=sub, n_heads=n_heads, hg=hg, page=page, t_new=t,
                          past=n_pages * page, lam_init=lam_init),
        grid_spec=grid_spec,
        out_shape=jax.ShapeDtypeStruct((b, t, hd), BF16),
        compiler_params=_params("arbitrary", "arbitrary"),
        name="diff_attention_sample",
    )(page_table, _alibi_slopes(n_heads), q, k_new, v_new, *lams, subln,
      *([split(cache_k)] * pps), *([split(cache_v)] * pps))


def _compress_body(*refs, n_ops, rows_op, nc, nc_pad, interleaved):
    page_refs = refs[:n_ops]
    pos_ref, w1_ref, w2_ref, o_ref, x_ref = refs[n_ops:]
    c, n_c = pl.program_id(1), pl.num_programs(1)
    blocks_op = rows_op // BLK
    for i, ref in enumerate(page_refs):
        for bi in range(blocks_op):
            blk = (c * n_ops + i) * blocks_op + bi
            r0 = pl.multiple_of(blk * BLK_PITCH, SUBLANES)
            for g in range(NSA_KV):
                if interleaved:
                    rows = ref[0, pl.ds(bi * BLK * NSA_KV + g, BLK, stride=NSA_KV), :]
                else:
                    rows = ref[0, bi * BLK:(bi + 1) * BLK, g * HEAD_DIM:(g + 1) * HEAD_DIM]
                x_ref[g, pl.ds(r0, BLK), :] = rows

    @pl.when(c == n_c - 1)
    def _():
        w1 = w1_ref[...].astype(BF16)
        w2 = w2_ref[...].astype(BF16)
        for g in range(NSA_KV):
            pieces = [(x_ref[g, pl.ds(j, nc, stride=BLK_PITCH), :] + pos_ref[j:j + 1, :]).astype(BF16)
                      for j in range(BLK)]
            a = jnp.concatenate(pieces, axis=1)
            mid = _silu(jnp.dot(a, w1, preferred_element_type=F32))
            out = jnp.dot(mid.astype(BF16), w2, preferred_element_type=F32)
            o_ref[0, g, 0:nc, :] = out
            if nc_pad > nc:
                o_ref[0, g, nc:nc_pad, :] = jnp.zeros((nc_pad - nc, HEAD_DIM), F32)


def compress(page_arrays, page_specs, grid, prefetch, length, pos, w1, w2, nc_pad, interleaved):
    n_ops = len(page_arrays)
    rows_op = length // (grid[1] * n_ops)
    nc = length // BLK
    n_pre = len(prefetch)
    const = lambda shape: pl.BlockSpec(shape, lambda bi, c, *pt: (0,) * len(shape))
    grid_spec = pltpu.PrefetchScalarGridSpec(
        num_scalar_prefetch=n_pre, grid=grid,
        in_specs=list(page_specs) + [const(pos.shape), const(w1.shape), const(w2.shape)],
        out_specs=pl.BlockSpec((1, NSA_KV, nc_pad, HEAD_DIM), lambda bi, c, *pt: (bi, 0, 0, 0)),
        scratch_shapes=[pltpu.VMEM((NSA_KV, nc * BLK_PITCH, HEAD_DIM), F32)])

    def body(*refs):
        _compress_body(*refs[n_pre:], n_ops=n_ops, rows_op=rows_op, nc=nc, nc_pad=nc_pad, interleaved=interleaved)

    return pl.pallas_call(
        body, grid_spec=grid_spec,
        out_shape=jax.ShapeDtypeStruct((grid[0], NSA_KV, nc_pad, HEAD_DIM), F32),
        compiler_params=_params("arbitrary", "arbitrary"),
        name="compress",
    )(*prefetch, *page_arrays, pos, w1, w2)


def compress_prompt(x, pos, w1, w2, nc_pad):
    b, t, w = x.shape
    spec = pl.BlockSpec((1, t, w), lambda bi, c: (bi, 0, 0))
    return compress([x], [spec], (b, 1), [], t, pos, w1, w2, nc_pad, interleaved=False)


def compress_paged(cache, page_table, pos, w1, w2, nc_pad, pages_per_step=32):
    b, n_pages = page_table.shape
    page = cache.shape[1] // NSA_KV
    pps = min(pages_per_step, n_pages)
    assert n_pages % pps == 0

    def spec(i):
        return pl.BlockSpec((1, page * NSA_KV, HEAD_DIM), lambda bi, c, pt: (pt[bi, c * pps + i], 0, 0))

    return compress([cache] * pps, [spec(i) for i in range(pps)], (b, n_pages // pps), [page_table],
                    n_pages * page, pos, w1, w2, nc_pad, interleaved=True)


def _select_blocks(imp, pos_q, n_blocks):
    t, w = imp.shape
    blk = lax.broadcasted_iota(jnp.int32, (t, w), 1)
    complete = blk * BLK + (BLK - 1) <= pos_q
    cur = blk == pos_q // BLK
    val = jnp.where(cur, BIG_SEL, jnp.where(blk == 0, 0.5 * BIG_SEL, jnp.where(complete, imp, NEG)))
    val = jnp.where(blk < n_blocks, val, -3e38)
    rank = jnp.zeros((t, w), jnp.int32)
    for i in range(n_blocks):
        vi = val[:, i:i + 1]
        ahead = (vi > val) | ((vi == val) & (blk > i))
        rank = rank + ahead.astype(jnp.int32)
    return (rank < min(N_SELECT, n_blocks)) & (blk < n_blocks)


def _expand_blocks(sel, first_block, n_keys):
    w = sel.shape[1]
    blk = lax.broadcasted_iota(jnp.int32, (w, n_keys), 0)
    key = lax.broadcasted_iota(jnp.int32, (w, n_keys), 1)
    onehot = jnp.where(blk == first_block + key // BLK, 1.0, 0.0).astype(BF16)
    return jnp.dot(sel, onehot, preferred_element_type=F32)


def _head_rows(q_ref, g, hpg):
    parts = [q_ref[0, :, (g * hpg + j) * HEAD_DIM:(g * hpg + j + 1) * HEAD_DIM] for j in range(hpg)]
    return (jnp.concatenate(parts, axis=0) * (HEAD_DIM ** -0.5 * LOG2E)).astype(BF16)


def _compressed_branch(q, kcb, vcb, slopes, pos_q, nc, hpg):
    t = pos_q.shape[0]
    w = kcb.shape[0]
    blk_end = lax.broadcasted_iota(jnp.int32, (t, w), 1) * BLK + (BLK - 1)
    dist = pos_q - blk_end
    mask = ((dist >= 0) & (blk_end < nc * BLK))[None]
    s = _nt_dot(q, kcb.astype(BF16)).reshape(hpg, t, w) - slopes * dist.astype(F32)[None]
    s = jnp.where(mask, s, NEG)
    e = jnp.where(mask, jnp.exp2(s - jnp.max(s, axis=-1, keepdims=True)), 0.0)
    p = e / jnp.maximum(jnp.sum(e, axis=-1, keepdims=True), 1e-30)
    o = jnp.dot(p.reshape(hpg * t, w).astype(BF16), vcb.astype(BF16), preferred_element_type=F32)
    return o, jnp.sum(p, axis=0)


def _merge_heads(o_ref, gates, g, hpg, o_cmp, o_slc, o_win):
    t = o_cmp.shape[0] // hpg
    for j in range(hpg):
        head = g * hpg + j
        rows = slice(j * t, (j + 1) * t)
        o = (gates[:, 3 * head:3 * head + 1] * o_cmp[rows]
             + gates[:, 3 * head + 1:3 * head + 2] * o_slc[rows]
             + gates[:, 3 * head + 2:3 * head + 3] * o_win[rows])
        o_ref[0, :, head * HEAD_DIM:(head + 1) * HEAD_DIM] = o.astype(o_ref.dtype)


def _nsa_prompt_body(slope_ref, q_ref, kcb_ref, vcb_ref, ks_ref, vs_ref, kw_ref, vw_ref, gate_ref, o_ref,
                     *, tq, ck, wlen, nc, hpg):
    qi = pl.program_id(1)
    q0 = qi * tq
    pos_q = q0 + lax.broadcasted_iota(jnp.int32, (tq, 1), 0)
    gates = gate_ref[0]
    rel_s = pos_q - lax.broadcasted_iota(jnp.int32, (tq, ck), 1)
    col_s = lax.broadcasted_iota(jnp.int32, (1, 1, ck), 2).astype(F32)
    w0 = pl.multiple_of(jnp.maximum(q0 + tq - wlen, 0), tq)
    dist_w = pos_q - (w0 + lax.broadcasted_iota(jnp.int32, (tq, wlen), 1))
    mask_w = ((dist_w >= 0) & (dist_w <= WINDOW))[None]
    pos_w = (w0 - q0 + lax.broadcasted_iota(jnp.int32, (1, 1, wlen), 2)).astype(F32)
    init = _softmax_init((hpg, tq), hpg * tq, HEAD_DIM)
    for g in range(NSA_KV):
        lanes = pl.ds(g * HEAD_DIM, HEAD_DIM)
        q = _head_rows(q_ref, g, hpg)
        slopes = _head_slopes(slope_ref, g * hpg, hpg)
        o_cmp, imp = _compressed_branch(q, kcb_ref[0, g], vcb_ref[0, g], slopes, pos_q, nc, hpg)
        sel = jnp.where(_select_blocks(imp, pos_q, nc), 1.0, 0.0).astype(BF16)

        def slc_chunk(c, carry, causal):
            k0 = pl.multiple_of(c * ck, ck)
            k = ks_ref[0, pl.ds(k0, ck), lanes].astype(BF16)
            v = vs_ref[0, pl.ds(k0, ck), lanes].astype(BF16)
            mask = _expand_blocks(sel, k0 // BLK, ck) > 0.5
            if causal:
                mask = mask & (rel_s >= k0)
            s = _nt_dot(q, k).reshape(hpg, tq, ck) + slopes * (col_s + (k0 - q0).astype(F32))
            return _softmax_update(carry, s, jnp.broadcast_to(mask[None], s.shape), v, zero_masked=False)

        n_full = q0 // ck
        carry = lax.fori_loop(0, n_full, functools.partial(slc_chunk, causal=False), init)
        carry = lax.fori_loop(n_full, (q0 + tq + ck - 1) // ck, functools.partial(slc_chunk, causal=True), carry)
        o_slc = _softmax_finish(carry)

        kw = kw_ref[0, pl.ds(w0, wlen), lanes].astype(BF16)
        vw = vw_ref[0, pl.ds(w0, wlen), lanes].astype(BF16)
        s = _nt_dot(q, kw).reshape(hpg, tq, wlen) + slopes * pos_w
        o_win = _softmax_finish(
            _softmax_update(init, s, jnp.broadcast_to(mask_w, s.shape), vw, zero_masked=False))
        _merge_heads(o_ref, gates, g, hpg, o_cmp, o_slc, o_win)


def nsa_attention_prompt(q, kcb, vcb, ks, vs, kw, vw, gates, tq=128, ck=512):
    b, t, hd = q.shape
    n_heads = hd // HEAD_DIM
    hpg = n_heads // NSA_KV
    tq, ck = min(tq, t), min(ck, t)
    wlen = min(WINDOW + tq, t)
    assert ck % tq == 0 and WINDOW % tq == 0 and N_SELECT >= 2
    tok = pl.BlockSpec((1, tq, hd), lambda bi, qi: (bi, qi, 0))
    cmp_spec = pl.BlockSpec((1,) + kcb.shape[1:], lambda bi, qi: (bi, 0, 0, 0))
    seq = pl.BlockSpec((1, t, NSA_KV * HEAD_DIM), lambda bi, qi: (bi, 0, 0))
    return pl.pallas_call(
        functools.partial(_nsa_prompt_body, tq=tq, ck=ck, wlen=wlen, nc=t // BLK, hpg=hpg),
        grid=(b, t // tq),
        in_specs=[_smem_spec(), tok, cmp_spec, cmp_spec, seq, seq, seq, seq,
                  pl.BlockSpec((1, tq, LANES), lambda bi, qi: (bi, qi, 0))],
        out_specs=tok,
        out_shape=jax.ShapeDtypeStruct((b, t, hd), BF16),
        compiler_params=_params("arbitrary", "arbitrary"),
        name="nsa_attention_prompt",
    )(_alibi_slopes(n_heads), q, kcb, vcb, ks, vs, kw, vw, gates)


def _nsa_sample_body(pt_ref, slope_ref, q_ref, kcb_ref, vcb_ref, ksn_ref, vsn_ref, kwn_ref, vwn_ref,
                     wk_ref, wv_ref, gate_ref, *rest, pps, page, t_new, past, hpg, sel_w):
    k_pages = rest[:pps]
    v_pages = rest[pps:2 * pps]
    o_ref = rest[2 * pps]
    q_s, ocmp_s, sel_s, m_s, l_s, acc_s = rest[2 * pps + 1:]
    c, n_c = pl.program_id(1), pl.num_programs(1)
    rows = hpg * t_new
    nc = past // BLK
    n_blocks = -(-(past + t_new) // BLK)
    pos_q = past + lax.broadcasted_iota(jnp.int32, (t_new, 1), 0)
    keys_step = pps * page

    @pl.when(c == 0)
    def _():
        for g in range(NSA_KV):
            q = _head_rows(q_ref, g, hpg)
            q_s[g] = q
            slopes = _head_slopes(slope_ref, g * hpg, hpg)
            o_cmp, imp = _compressed_branch(q, kcb_ref[0, g], vcb_ref[0, g], slopes, pos_q, nc, hpg)
            ocmp_s[g] = o_cmp
            imp = jnp.concatenate([imp, jnp.zeros((t_new, sel_w - imp.shape[1]), F32)], axis=1)
            sel_s[g] = jnp.where(_select_blocks(imp, pos_q, n_blocks), 1.0, 0.0)
        m_s[...] = jnp.full(m_s.shape, NEG, F32)
        l_s[...] = jnp.zeros(l_s.shape, F32)
        acc_s[...] = jnp.zeros(acc_s.shape, F32)

    def group_rows(ref, g):
        return ref[0, pl.ds(g, page, stride=NSA_KV), :]

    rel = pos_q - lax.broadcasted_iota(jnp.int32, (t_new, keys_step), 1)
    for g in range(NSA_KV):
        slopes = _head_slopes(slope_ref, g * hpg, hpg)
        q = q_s[g]
        k0 = c * keys_step
        dist = rel - k0
        mask = (_expand_blocks(sel_s[g].astype(BF16), k0 // BLK, keys_step) > 0.5)[None]
        k = jnp.concatenate([group_rows(k_pages[i], g) for i in range(pps)], axis=0).astype(BF16)
        v = jnp.concatenate([group_rows(v_pages[i], g) for i in range(pps)], axis=0).astype(BF16)
        s = _nt_dot(q, k).reshape(hpg, t_new, keys_step) - slopes * dist.astype(F32)[None]
        carry = _softmax_update((m_s[g], l_s[g], acc_s[g]), s, jnp.broadcast_to(mask, s.shape), v)
        m_s[g], l_s[g], acc_s[g] = carry

    @pl.when(c == n_c - 1)
    def _():
        gates = gate_ref[0]
        pad = jnp.zeros((page - t_new, HEAD_DIM), F32)
        col = lax.broadcasted_iota(jnp.int32, (t_new, page), 1)
        dist_new = (pos_q - past) - col
        for g in range(NSA_KV):
            lanes = pl.ds(g * HEAD_DIM, HEAD_DIM)
            slopes = _head_slopes(slope_ref, g * hpg, hpg)
            q = q_s[g]
            k = jnp.concatenate([ksn_ref[0, :, lanes], pad], axis=0).astype(BF16)
            v = jnp.concatenate([vsn_ref[0, :, lanes], pad], axis=0).astype(BF16)
            sel_new = _expand_blocks(sel_s[g].astype(BF16), nc, page) > 0.5
            mask = (sel_new & (dist_new >= 0) & (col < t_new))[None]
            s = _nt_dot(q, k).reshape(hpg, t_new, page) - slopes * dist_new.astype(F32)[None]
            carry = _softmax_update((m_s[g], l_s[g], acc_s[g]), s, jnp.broadcast_to(mask, s.shape), v)
            o_slc = _softmax_finish(carry)
            w_buf = wk_ref.shape[1] // NSA_KV
            win_rows = pl.ds(g, w_buf, stride=NSA_KV)
            colw = lax.broadcasted_iota(jnp.int32, (t_new, w_buf), 1)
            dist_w = pos_q - (past - w_buf + colw)
            carry = _softmax_init((hpg, t_new), rows, HEAD_DIM)
            sw = _nt_dot(q, wk_ref[0, win_rows, :].astype(BF16)).reshape(hpg, t_new, w_buf)
            sw = sw - slopes * dist_w.astype(F32)[None]
            mask_w = ((dist_w >= 0) & (dist_w <= WINDOW))[None]
            carry = _softmax_update(carry, sw, jnp.broadcast_to(mask_w, sw.shape),
                                    wv_ref[0, win_rows, :].astype(BF16))
            k = jnp.concatenate([kwn_ref[0, :, lanes], pad], axis=0).astype(BF16)
            v = jnp.concatenate([vwn_ref[0, :, lanes], pad], axis=0).astype(BF16)
            mask_n = ((dist_new >= 0) & (dist_new <= WINDOW) & (col < t_new))[None]
            sn = _nt_dot(q, k).reshape(hpg, t_new, page) - slopes * dist_new.astype(F32)[None]
            carry = _softmax_update(carry, sn, jnp.broadcast_to(mask_n, sn.shape), v)
            o_win = _softmax_finish(carry)
            _merge_heads(o_ref, gates, g, hpg, ocmp_s[g], o_slc, o_win)


def nsa_attention_sample(q, kcb, vcb, ks_new, vs_new, kw_new, vw_new, win_k, win_v, gates,
                         cache_k, cache_v, page_table, pages_per_step=16):
    b, t, hd = q.shape
    n_heads = hd // HEAD_DIM
    hpg = n_heads // NSA_KV
    n_pages = page_table.shape[1]
    page = cache_k.shape[1] // NSA_KV
    past = n_pages * page
    pps = min(pages_per_step, n_pages)
    n_blocks = -(-(past + t) // BLK)
    sel_w = -(-n_blocks // LANES) * LANES
    assert n_pages % pps == 0 and t <= page and page % BLK == 0 and kcb.shape[2] * BLK >= past
    width = NSA_KV * HEAD_DIM
    tok = pl.BlockSpec((1, t, hd), lambda bi, c, pt: (bi, 0, 0))
    new = pl.BlockSpec((1, t, width), lambda bi, c, pt: (bi, 0, 0))
    cmp_spec = pl.BlockSpec((1,) + kcb.shape[1:], lambda bi, c, pt: (bi, 0, 0, 0))
    win = pl.BlockSpec((1,) + win_k.shape[1:], lambda bi, c, pt: (bi, 0, 0))

    def page_spec(i):
        return pl.BlockSpec((1, page * NSA_KV, HEAD_DIM), lambda bi, c, pt: (pt[bi, c * pps + i], 0, 0))

    pages = [page_spec(i) for i in range(pps)]
    rows = hpg * t
    grid_spec = pltpu.PrefetchScalarGridSpec(
        num_scalar_prefetch=1,
        grid=(b, n_pages // pps),
        in_specs=[_smem_spec(), tok, cmp_spec, cmp_spec, new, new, new, new, win, win,
                  pl.BlockSpec((1, t, LANES), lambda bi, c, pt: (bi, 0, 0))] + pages + pages,
        out_specs=tok,
        scratch_shapes=[pltpu.VMEM((NSA_KV, rows, HEAD_DIM), BF16),
                        pltpu.VMEM((NSA_KV, rows, HEAD_DIM), F32),
                        pltpu.VMEM((NSA_KV, t, sel_w), F32),
                        pltpu.VMEM((NSA_KV, hpg, t, 1), F32),
                        pltpu.VMEM((NSA_KV, hpg, t, 1), F32),
                        pltpu.VMEM((NSA_KV, rows, HEAD_DIM), F32)])
    return pl.pallas_call(
        functools.partial(_nsa_sample_body, pps=pps, page=page, t_new=t, past=past, hpg=hpg, sel_w=sel_w),
        grid_spec=grid_spec,
        out_shape=jax.ShapeDtypeStruct((b, t, hd), BF16),
        compiler_params=_params("arbitrary", "arbitrary"),
        name="nsa_attention_sample",
    )(page_table, _alibi_slopes(n_heads), q, kcb, vcb, ks_new, vs_new, kw_new, vw_new, win_k, win_v, gates,
      *([cache_k] * pps), *([cache_v] * pps))


def _last_rows(a, n):
    a = jnp.pad(a, ((0, 0), (n, 0)) + ((0, 0),) * (a.ndim - 2))
    return a[:, a.shape[1] - n:]


def _project(x, mods, norm_g, w_in_t, mix_diff, mix_nsa):
    b, t, d = x.shape
    h = modulate(x, mods[0], mods[1], norm_g)
    seg = lambda col0, n: matmul(h, w_in_t, col0=col0, ncols=n, wt=True).reshape(b, t, n)
    dq, dk, dv = seg(0, mix_diff), seg(mix_diff, mix_diff), seg(2 * mix_diff, mix_diff)
    nq = seg(3 * mix_diff, mix_nsa)
    n_gate = 3 * (mix_nsa // HEAD_DIM)
    outs = kv_gate_projection(h, w_in_t, 3 * mix_diff + mix_nsa, 6, n_gate)
    kvs = [o.reshape(b, t, NSA_KV * HEAD_DIM) for o in outs[:6]]
    gates = outs[6].reshape(b, t, LANES)
    return dq, dk, dv, nq, kvs, gates


def _finish(x, o_diff, o_nsa, mods, norm2_g, w_out, w_up, w_down, final_g):
    b, t, d = x.shape
    mix = matmul_split(o_diff.reshape(b * t, -1), o_nsa.reshape(b * t, -1), w_out)
    x1, h2 = residual_modulate(x, mix, mods[2], mods[3], mods[4], norm2_g)
    u = matmul(h2, w_up, out_dtype=BF16, relu2=True)
    ffn = matmul(u, w_down, tn=1024, tk=2048)
    return residual_final_norm(x1, ffn, mods[5], final_g)


def kernel(x_prompt, x_sample, c_prompt, c_sample, cache_diff_k, cache_diff_v, cache_cmp_k, cache_cmp_v, cache_slc_k, cache_slc_v, state_win_k, state_win_v, page_table, norm1_g, norm2_g, ada_w, ada_b, w_in, w_out, diff_lq1, diff_lk1, diff_lq2, diff_lk2, diff_subln_g, cmp_k_pos, cmp_k_w1, cmp_k_w2, cmp_v_pos, cmp_v_w1, cmp_v_w2, w_up, w_down, final_g):
    depth = ada_w.shape[0]
    assert depth == 1
    l = 0
    bp, tp, d = x_prompt.shape
    bs, ts, _ = x_sample.shape
    n_phys, page, n_heads_diff, _ = cache_diff_k.shape[1:]
    mix_diff = n_heads_diff * HEAD_DIM
    mix_nsa = d - mix_diff
    w_buf = state_win_k.shape[2]
    lam_init = 0.8 - 0.6 * math.exp(-0.3 * l)
    lams = [a[l].reshape(1, DIFF_HALF) for a in (diff_lq1, diff_lk1, diff_lq2, diff_lk2)]
    subln = diff_subln_g[l].reshape(1, HEAD_DIM)

    n_c = bp + bs
    n_c_pad = -(-n_c // 16) * 16
    c_all = jnp.concatenate([c_prompt, c_sample, jnp.zeros((n_c_pad - n_c, d), F32)], axis=0)
    mod = ada_modulation(c_all, ada_w[l], ada_b[l]).reshape(n_c_pad, 6, 1, d)
    mods_p = [mod[:bp, i] for i in range(6)]
    mods_s = [mod[bp:n_c, i] for i in range(6)]

    w_in_t = jnp.swapaxes(w_in[l], 0, 1)

    dq, dk_p, dv_p, nq, kvs_p, gates = _project(x_prompt, mods_p, norm1_g[l], w_in_t, mix_diff, mix_nsa)
    o_diff = diff_attention_prompt(dq, dk_p, dv_p, lams, subln, lam_init)
    nc_pad = -(-max(tp, page_table.shape[1] * page) // (BLK * LANES)) * LANES
    kcb = compress_prompt(kvs_p[0], cmp_k_pos[l], cmp_k_w1[l], cmp_k_w2[l], nc_pad)
    vcb = compress_prompt(kvs_p[1], cmp_v_pos[l], cmp_v_w1[l], cmp_v_w2[l], nc_pad)
    o_nsa = nsa_attention_prompt(nq, kcb, vcb, kvs_p[2], kvs_p[3], kvs_p[4], kvs_p[5], gates)
    y_prompt = _finish(x_prompt, o_diff, o_nsa, mods_p, norm2_g[l], w_out[l], w_up[l], w_down[l], final_g)

    dq, dk_s, dv_s, nq, kvs_s, gates = _project(x_sample, mods_s, norm1_g[l], w_in_t, mix_diff, mix_nsa)
    o_diff = diff_attention_sample(dq, dk_s, dv_s, cache_diff_k[l], cache_diff_v[l], page_table, lams, subln, lam_init)
    flat_nsa = lambda c: c[l].reshape(n_phys, page * NSA_KV, HEAD_DIM)
    kcb = compress_paged(flat_nsa(cache_cmp_k), page_table, cmp_k_pos[l], cmp_k_w1[l], cmp_k_w2[l], nc_pad)
    vcb = compress_paged(flat_nsa(cache_cmp_v), page_table, cmp_v_pos[l], cmp_v_w1[l], cmp_v_w2[l], nc_pad)
    win_k = state_win_k[l].reshape(bs, w_buf * NSA_KV, HEAD_DIM)
    win_v = state_win_v[l].reshape(bs, w_buf * NSA_KV, HEAD_DIM)
    o_nsa = nsa_attention_sample(nq, kcb, vcb, kvs_s[2], kvs_s[3], kvs_s[4], kvs_s[5], win_k, win_v, gates,
                                 flat_nsa(cache_slc_k), flat_nsa(cache_slc_v), page_table)
    y_sample = _finish(x_sample, o_diff, o_nsa, mods_s, norm2_g[l], w_out[l], w_up[l], w_down[l], final_g)

    heads = lambda a, b, t, h: a.reshape(1, b, t, h, HEAD_DIM)
    out_p = [heads(dk_p, bp, tp, n_heads_diff), heads(dv_p, bp, tp, n_heads_diff)]
    out_p += [heads(a, bp, tp, NSA_KV) for a in kvs_p[:4]]
    out_p += [heads(_last_rows(a, w_buf), bp, w_buf, NSA_KV) for a in kvs_p[4:]]
    out_s = [heads(dk_s, bs, ts, n_heads_diff), heads(dv_s, bs, ts, n_heads_diff)]
    out_s += [heads(a, bs, ts, NSA_KV) for a in kvs_s[:4]]
    out_s += [jnp.concatenate([w[l], a.reshape(bs, ts, NSA_KV, HEAD_DIM)], axis=1)[None, :, ts:]
              for w, a in ((state_win_k, kvs_s[4]), (state_win_v, kvs_s[5]))]
    return (y_prompt, y_sample, *out_p, *out_s)
```
